```python
import math
import jax, jax.numpy as jnp
from jax import lax
import numpy as np

D_MODEL = 1024
BATCH = 2
SEQ = 8192
DEPTH = 1
DEC_BATCH = 128
DEC_SEQ = 1
PAST_LEN = 8192
PAGE_SIZE = 128

HEAD_DIM = 64
A_HEADS = 8
A_KV_HEADS = 2
B_HEADS = 8
B_KV_HEADS = 2
A_WIDTH = A_HEADS * HEAD_DIM
B_WIDTH = B_HEADS * HEAD_DIM
IDX_HEADS = 8
IDX_DIM = 64
DSA_TOPK = 256
MOBA_BLOCK = 256
MOBA_TOPK = 3
ROPE_THETA = 10000.0
QBLK = 128
EPS = 1e-6

SPLIT_SIZES = (
    A_WIDTH,
    A_KV_HEADS * HEAD_DIM,
    A_KV_HEADS * HEAD_DIM,
    A_WIDTH,
    IDX_HEADS * IDX_DIM,
    IDX_DIM,
    IDX_HEADS,
    B_WIDTH,
    B_KV_HEADS * HEAD_DIM,
    B_KV_HEADS * HEAD_DIM,
    B_WIDTH,
    D_MODEL,
    D_MODEL,
)
IN_WIDTH = sum(SPLIT_SIZES)

kernel_name = "dsa_moba_parallel_gated_decoder_step"


def rms_norm(x):
    xf = x.astype(jnp.float32)
    y = xf * lax.rsqrt(jnp.mean(xf * xf, axis=-1, keepdims=True) + EPS)
    return y.astype(x.dtype)


def rope(x, pos):
    half = x.shape[-1] // 2
    freqs = ROPE_THETA ** (-jnp.arange(half, dtype=jnp.float32) / half)
    ang = pos.astype(jnp.float32)[:, None] * freqs[None, :]
    cos = jnp.cos(ang)[None, :, None, :]
    sin = jnp.sin(ang)[None, :, None, :]
    x1 = x[..., :half].astype(jnp.float32)
    x2 = x[..., half:].astype(jnp.float32)
    return jnp.concatenate([x1 * cos - x2 * sin, x2 * cos + x1 * sin], axis=-1).astype(x.dtype)


def prepare(x, c, pos, w_ada, b_ada, w_in):
    N, T, _ = x.shape
    ada = c @ w_ada + b_ada
    shift, scale, gate = jnp.split(ada, 3, axis=-1)
    h = rms_norm(x) * (1 + scale[:, None, :]) + shift[:, None, :]
    u = h @ w_in
    offsets = [int(o) for o in np.cumsum(SPLIT_SIZES)[:-1]]
    (q_a, k_a, v_a, z_a, q_i, k_i, w_i, q_b, k_b, v_b, z_b, g_a, g_b) = jnp.split(u, offsets, axis=-1)
    heads = lambda t, nh: t.reshape(N, T, nh, -1)
    return dict(
        q_a=rope(heads(q_a, A_HEADS), pos), k_a=rope(heads(k_a, A_KV_HEADS), pos), v_a=heads(v_a, A_KV_HEADS),
        q_i=rope(heads(q_i, IDX_HEADS), pos), k_i=rope(k_i[:, :, None, :], pos)[:, :, 0, :], w_i=w_i,
        q_b=rope(heads(q_b, B_HEADS), pos), k_b=rope(heads(k_b, B_KV_HEADS), pos), v_b=heads(v_b, B_KV_HEADS),
        z_a=z_a, z_b=z_b, g_a=g_a, g_b=g_b, gate=gate)


def dsa_attend(q, w_i, q_i, pos, k, v, k_i, ktop):
    N, Q, H, D = q.shape
    L, KV = k.shape[1], k.shape[2]
    G = H // KV
    logits = jnp.einsum('nqhd,nsd->nqhs', q_i.astype(jnp.float32), k_i.astype(jnp.float32)) * (IDX_DIM ** -0.5)
    score = jnp.einsum('nqh,nqhs->nqs', w_i.astype(jnp.float32) * (IDX_HEADS ** -0.5), jax.nn.relu(logits))
    causal = jnp.arange(L)[None, :] <= pos[:, None]
    score = jnp.where(causal[None], score, -jnp.inf)
    _, sel = lax.top_k(score, ktop)
    valid = sel <= pos[None, :, None]
    n_idx = jnp.arange(N)[:, None, None]
    ks = k[n_idx, sel].astype(jnp.float32)
    vs = v[n_idx, sel].astype(jnp.float32)
    qg = q.astype(jnp.float32).reshape(N, Q, KV, G, D)
    s = jnp.einsum('nqkgd,nqjkd->nqkgj', qg, ks) * (D ** -0.5)
    s = jnp.where(valid[:, :, None, None, :], s, -jnp.inf)
    p = jax.nn.softmax(s, axis=-1)
    o = jnp.einsum('nqkgj,nqjkd->nqkgd', p, vs)
    return o.reshape(N, Q, H * D).astype(q.dtype)


def block_means(k):
    N, L, KV, D = k.shape
    nb = -(-L // MOBA_BLOCK)
    kp = jnp.pad(k.astype(jnp.float32), ((0, 0), (0, nb * MOBA_BLOCK - L), (0, 0), (0, 0)))
    return kp.reshape(N, nb, MOBA_BLOCK, KV, D).mean(axis=2)


def moba_attend(q, pos, k, v, kmean):
    N, Q, H, D = q.shape
    L, KV = k.shape[1], k.shape[2]
    NB = kmean.shape[1]
    G = H // KV
    qg = q.astype(jnp.float32).reshape(N, Q, KV, G, D)
    gate = jnp.einsum('nqkgd,nbkd->nqkgb', qg, kmean)
    own = pos // MOBA_BLOCK
    past_ok = jnp.arange(NB)[None, :] < own[:, None]
    gate = jnp.where(past_ok[None, :, None, None, :], gate, -jnp.inf)
    kb = min(MOBA_TOPK, NB)
    _, blk = lax.top_k(gate, kb)
    blk_ok = blk < own[None, :, None, None, None]
    offs = jnp.arange(MOBA_BLOCK)
    J = kb * MOBA_BLOCK
    pidx = jnp.minimum((blk[..., None] * MOBA_BLOCK + offs).reshape(N, Q, KV, G, J), L - 1)
    n_i = jnp.arange(N)[:, None, None, None, None]
    kv_i = jnp.arange(KV)[None, None, :, None, None]
    kp = k[n_i, pidx, kv_i].astype(jnp.float32)
    vp = v[n_i, pidx, kv_i].astype(jnp.float32)
    s_past = jnp.einsum('nqkgd,nqkgjd->nqkgj', qg, kp)
    s_past = jnp.where(jnp.repeat(blk_ok, MOBA_BLOCK, axis=-1), s_past, -jnp.inf)
    oidx = own[:, None] * MOBA_BLOCK + offs[None, :]
    own_ok = oidx <= pos[:, None]
    oidx = jnp.minimum(oidx, L - 1)
    ko = k[:, oidx].astype(jnp.float32)
    vo = v[:, oidx].astype(jnp.float32)
    s_own = jnp.einsum('nqkgd,nqjkd->nqkgj', qg, ko)
    s_own = jnp.where(own_ok[None, :, None, None, :], s_own, -jnp.inf)
    p = jax.nn.softmax(jnp.concatenate([s_past, s_own], axis=-1) * (D ** -0.5), axis=-1)
    o = (jnp.einsum('nqkgj,nqkgjd->nqkgd', p[..., :J], vp)
         + jnp.einsum('nqkgj,nqjkd->nqkgd', p[..., J:], vo))
    return o.reshape(N, Q, H * D).astype(q.dtype)


def prompt_mix(pp, pos, ktop):
    N, T = pp['q_a'].shape[:2]
    kmean = block_means(pp['k_b'])

    def body(ci):
        start = ci * QBLK
        sl = lambda a: lax.dynamic_slice_in_dim(a, start, QBLK, axis=1)
        p = lax.dynamic_slice_in_dim(pos, start, QBLK)
        oa = dsa_attend(sl(pp['q_a']), sl(pp['w_i']), sl(pp['q_i']), p, pp['k_a'], pp['v_a'], pp['k_i'], ktop)
        ob = moba_attend(sl(pp['q_b']), p, pp['k_b'], pp['v_b'], kmean)
        return oa, ob

    oa, ob = lax.map(body, jnp.arange(T // QBLK))
    oa = jnp.moveaxis(oa, 0, 1).reshape(N, T, A_WIDTH)
    ob = jnp.moveaxis(ob, 0, 1).reshape(N, T, B_WIDTH)
    return oa, ob


def finish(x, p, oa, ob, w_br_a, w_br_b, w_o):
    ya = (oa * jax.nn.silu(p['z_a'])) @ w_br_a
    yb = (ob * jax.nn.silu(p['z_b'])) @ w_br_b
    m = jax.nn.sigmoid(p['g_a']) * ya + jax.nn.sigmoid(p['g_b']) * yb
    return x + p['gate'][:, None, :] * (m @ w_o)


def setup_inputs(seed: int = 0) -> dict:
    key = jax.random.key(seed)
    ks = jax.random.split(key, 20)
    f32 = jnp.float32
    nrm = lambda k, shape, s: jax.random.normal(k, shape, f32) * s
    n_pages = PAST_LEN // PAGE_SIZE
    n_used = DEC_BATCH * n_pages
    n_pool = n_used + max(1, n_used // 4)
    perm = jax.random.permutation(ks[0], n_pool)[:n_used]
    page_table = perm.reshape(DEC_BATCH, n_pages).astype(jnp.int32)
    kv_a = (DEPTH, n_pool, PAGE_SIZE, A_KV_HEADS, HEAD_DIM)
    kv_b = (DEPTH, n_pool, PAGE_SIZE, B_KV_HEADS, HEAD_DIM)
    return {
        "x_prompt": nrm(ks[1], (BATCH, SEQ, D_MODEL), 1.0),
        "x_sample": nrm(ks[2], (DEC_BATCH, DEC_SEQ, D_MODEL), 1.0),
        "cache_k_a": nrm(ks[3], kv_a, 1.0),
        "cache_v_a": nrm(ks[4], kv_a, 1.0),
        "cache_k_idx": nrm(ks[5], (DEPTH, n_pool, PAGE_SIZE, IDX_DIM), 1.0),
        "cache_k_b": nrm(ks[6], kv_b, 1.0),
        "cache_v_b": nrm(ks[7], kv_b, 1.0),
        "page_table": page_table,
        "c_prompt": nrm(ks[8], (BATCH, D_MODEL), 1.0),
        "c_sample": nrm(ks[9], (DEC_BATCH, D_MODEL), 1.0),
        "w_ada": nrm(ks[10], (DEPTH, D_MODEL, 3 * D_MODEL), 0.5 * D_MODEL ** -0.5),
        "b_ada": nrm(ks[11], (DEPTH, 3 * D_MODEL), 0.02),
        "w_in": nrm(ks[12], (DEPTH, D_MODEL, IN_WIDTH), D_MODEL ** -0.5),
        "w_br_a": nrm(ks[13], (DEPTH, A_WIDTH, D_MODEL), A_WIDTH ** -0.5),
        "w_br_b": nrm(ks[14], (DEPTH, B_WIDTH, D_MODEL), B_WIDTH ** -0.5),
        "w_o": nrm(ks[15], (DEPTH, D_MODEL, D_MODEL), D_MODEL ** -0.5),
        "g_final": 1.0 + nrm(ks[16], (D_MODEL,), 0.02),
    }


def reference(x_prompt, x_sample, cache_k_a, cache_v_a, cache_k_idx, cache_k_b, cache_v_b, page_table,
              c_prompt, c_sample, w_ada, b_ada, w_in, w_br_a, w_br_b, w_o, g_final):
    T = x_prompt.shape[1]
    NS, S = x_sample.shape[0], x_sample.shape[1]
    n_pages = page_table.shape[1]
    past_len = n_pages * PAGE_SIZE
    pos_p = jnp.arange(T, dtype=jnp.int32)
    pos_s = past_len + jnp.arange(S, dtype=jnp.int32)
    ktop_p = min(DSA_TOPK, T // 4)
    ktop_s = min(DSA_TOPK, (past_len + S) // 4)
    xp, xs = x_prompt, x_sample
    rows_p = {n: [] for n in ('k_a', 'v_a', 'k_i', 'k_b', 'v_b')}
    rows_s = {n: [] for n in ('k_a', 'v_a', 'k_i', 'k_b', 'v_b')}
    for l in range(DEPTH):
        pp = prepare(xp, c_prompt, pos_p, w_ada[l], b_ada[l], w_in[l])
        oa_p, ob_p = prompt_mix(pp, pos_p, ktop_p)
        xp = finish(xp, pp, oa_p, ob_p, w_br_a[l], w_br_b[l], w_o[l])
        ps = prepare(xs, c_sample, pos_s, w_ada[l], b_ada[l], w_in[l])

        def full(cache, new):
            past = cache[l][page_table].reshape((NS, past_len) + cache.shape[3:])
            return jnp.concatenate([past.astype(new.dtype), new], axis=1)

        ka, va, ki = full(cache_k_a, ps['k_a']), full(cache_v_a, ps['v_a']), full(cache_k_idx, ps['k_i'])
        kb, vb = full(cache_k_b, ps['k_b']), full(cache_v_b, ps['v_b'])
        oa_s = dsa_attend(ps['q_a'], ps['w_i'], ps['q_i'], pos_s, ka, va, ki, ktop_s)
        ob_s = moba_attend(ps['q_b'], pos_s, kb, vb, block_means(kb))
        xs = finish(xs, ps, oa_s, ob_s, w_br_a[l], w_br_b[l], w_o[l])
        for n, src in (('k_a', 'k_a'), ('v_a', 'v_a'), ('k_i', 'k_i'), ('k_b', 'k_b'), ('v_b', 'v_b')):
            rows_p[n].append(pp[src])
            rows_s[n].append(ps[src])
    y_prompt = rms_norm(xp) * g_final
    y_sample = rms_norm(xs) * g_final
    return (y_prompt, y_sample,
            jnp.stack(rows_p['k_a']), jnp.stack(rows_p['v_a']), jnp.stack(rows_p['k_i']),
            jnp.stack(rows_p['k_b']), jnp.stack(rows_p['v_b']),
            jnp.stack(rows_s['k_a']), jnp.stack(rows_s['v_a']), jnp.stack(rows_s['k_i']),
            jnp.stack(rows_s['k_b']), jnp.stack(rows_s['v_b']))
```

```python
import functools

import jax
import jax.numpy as jnp
from jax import lax
from jax.experimental import pallas as pl
from jax.experimental.pallas import tpu as pltpu

F32 = jnp.float32
BF16 = jnp.bfloat16

D_MODEL = 1024
HEAD_DIM = 64
N_HEADS = 8
KV_HEADS = 2
GROUP = N_HEADS // KV_HEADS
IDX_DIM = 64
BR_WIDTH = N_HEADS * HEAD_DIM
DSA_TOPK = 256
MOBA_BLOCK = 256
MOBA_TOPK = 3
PAGE = 128
ROPE_THETA = 10000.0
EPS = 1e-6
LANES = 128
CK = 256
TQ = 128
NEG = -1e30
DEC_CHUNK = 1024

_ORIG_SPLIT = 1864
_PAD_COLS = 56
C_QA, C_KA, C_VA, C_ZA, C_QI, C_KIW, C_QB, C_KB, C_VB, C_ZB, C_GA, C_GB, C_END = (
    0, 512, 640, 768, 1280, 1792, 1920, 2432, 2560, 2688, 3200, 4224, 5248)

_NT = (((1,), (1,)), ((), ()))


def _dot_nt(a, b):
    return lax.dot_general(a, b, _NT, preferred_element_type=F32)


def _dot(a, b):
    return jnp.dot(a, b, preferred_element_type=F32)


def _ada_kernel(c_ref, w_ref, b_ref, o_ref):
    o_ref[...] = _dot(c_ref[...].astype(BF16), w_ref[...].astype(BF16)) + b_ref[...]


def _ada(c_all, w_ada, b_ada):
    rows = c_all.shape[0]
    return pl.pallas_call(
        _ada_kernel,
        grid=(3,),
        in_specs=[pl.BlockSpec((rows, D_MODEL), lambda j: (0, 0)),
                  pl.BlockSpec((D_MODEL, D_MODEL), lambda j: (0, j)),
                  pl.BlockSpec((1, D_MODEL), lambda j: (0, j))],
        out_specs=pl.BlockSpec((rows, D_MODEL), lambda j: (0, j)),
        out_shape=jax.ShapeDtypeStruct((rows, 3 * D_MODEL), F32),
        name="ada",
    )(c_all, w_ada, b_ada)


def _rope(x, cos, sin_signed):
    w = x.shape[1]
    reps = w // LANES
    lane = lax.broadcasted_iota(jnp.int32, x.shape, 1)
    upper = (lane % HEAD_DIM) >= (HEAD_DIM // 2)
    from_below = pltpu.roll(x, HEAD_DIM // 2, axis=1)
    from_above = pltpu.roll(x, w - HEAD_DIM // 2, axis=1)
    partner = jnp.where(upper, from_below, from_above)
    if reps > 1:
        cos = jnp.concatenate([cos] * reps, axis=1)
        sin_signed = jnp.concatenate([sin_signed] * reps, axis=1)
    return x * cos + partner * sin_signed


def _store_padded_heads(q, out_ref, low_for_head):
    lane = lax.broadcasted_iota(jnp.int32, (q.shape[0], LANES), 1)
    is_low = lane < HEAD_DIM
    for h in range(N_HEADS):
        col = q[:, (h // 2) * LANES:(h // 2 + 1) * LANES]
        want_low = low_for_head(h)
        if (h % 2 == 0) != want_low:
            col = pltpu.roll(col, HEAD_DIM, axis=1)
        keep = is_low if want_low else jnp.logical_not(is_low)
        out_ref[h] = jnp.where(keep, col, 0.0).astype(BF16)


def _prep_kernel(x_ref, ada_ref, cos_ref, sin_ref, w_ref,
                 qa_ref, qi_ref, qb_ref,
                 ka32_ref, va32_ref, kiw32_ref, kb32_ref, vb32_ref,
                 ka16_ref, va16_ref, kiw16_ref, kb16_ref, vb16_ref,
                 za_ref, zb_ref, ga_ref, gb_ref, km_ref):
    x = x_ref[...]
    ada = ada_ref[...]
    shift = ada[:, :D_MODEL]
    scale = ada[:, D_MODEL:2 * D_MODEL]
    ms = jnp.mean(x * x, axis=-1, keepdims=True)
    h = (x * lax.rsqrt(ms + EPS)) * (1.0 + scale) + shift
    hb = h.astype(BF16)
    cos = cos_ref[...]
    sin = sin_ref[...]

    def proj(a, b):
        return _dot(hb, w_ref[:, a:b])

    kv_low = lambda hh: hh < GROUP
    qscale = HEAD_DIM ** -0.5
    _store_padded_heads(_rope(proj(C_QA, C_KA), cos, sin) * qscale, qa_ref, kv_low)
    _store_padded_heads(_rope(proj(C_QI, C_KIW), cos, sin) * (IDX_DIM ** -0.5), qi_ref,
                        lambda hh: True)
    _store_padded_heads(_rope(proj(C_QB, C_KB), cos, sin) * qscale, qb_ref, kv_low)

    ka = _rope(proj(C_KA, C_VA), cos, sin)
    ka32_ref[...] = ka
    ka16_ref[...] = ka.astype(BF16)
    va = proj(C_VA, C_ZA)
    va32_ref[...] = va
    va16_ref[...] = va.astype(BF16)

    kiw = proj(C_KIW, C_QB)
    lane = lax.broadcasted_iota(jnp.int32, kiw.shape, 1)
    kiw = jnp.where(lane < IDX_DIM, _rope(kiw, cos, sin), kiw * (N_HEADS ** -0.5))
    kiw32_ref[...] = kiw
    kiw16_ref[...] = kiw.astype(BF16)

    kb = _rope(proj(C_KB, C_VB), cos, sin)
    kb32_ref[...] = kb
    kb16_ref[...] = kb.astype(BF16)
    km_ref[0] = jnp.sum(kb, axis=0, keepdims=True) * (1.0 / MOBA_BLOCK)
    vb = proj(C_VB, C_ZB)
    vb32_ref[...] = vb
    vb16_ref[...] = vb.astype(BF16)

    za_ref[...] = jax.nn.silu(proj(C_ZA, C_QI)).astype(BF16)
    zb_ref[...] = jax.nn.silu(proj(C_ZB, C_GA)).astype(BF16)
    ga_ref[...] = jax.nn.sigmoid(proj(C_GA, C_GB)).astype(BF16)
    gb_ref[...] = jax.nn.sigmoid(proj(C_GB, C_END)).astype(BF16)


def _prepare(x, ada, cos, sin, w_packed, tm, rows_per_ada):
    rows = x.shape[0]
    nt = rows // tm
    tab_tiles = cos.shape[0] // tm
    tiles_per_ada = rows_per_ada // tm
    a_rows = ada.shape[1]
    row_spec = lambda w: pl.BlockSpec((tm, w), lambda i: (i, 0))
    head_spec = pl.BlockSpec((N_HEADS, tm, LANES), lambda i: (0, i, 0))
    sd = jax.ShapeDtypeStruct
    out_shape = (
        [sd((N_HEADS, rows, LANES), BF16)] * 3
        + [sd((rows, LANES), F32)] * 5
        + [sd((rows, LANES), BF16)] * 5
        + [sd((rows, BR_WIDTH), BF16)] * 2
        + [sd((rows, D_MODEL), BF16)] * 2
        + [sd((nt, 1, LANES), F32)])
    out_specs = (
        [head_spec] * 3 + [row_spec(LANES)] * 10 + [row_spec(BR_WIDTH)] * 2
        + [row_spec(D_MODEL)] * 2 + [pl.BlockSpec((1, 1, LANES), lambda i: (i, 0, 0))])
    return pl.pallas_call(
        _prep_kernel,
        grid=(nt,),
        in_specs=[row_spec(D_MODEL),
                  pl.BlockSpec((None, a_rows, 3 * D_MODEL), lambda i: (i // tiles_per_ada, 0, 0)),
                  pl.BlockSpec((tm, LANES), lambda i: (i % tab_tiles, 0)),
                  pl.BlockSpec((tm, LANES), lambda i: (i % tab_tiles, 0)),
                  pl.BlockSpec((D_MODEL, C_END), lambda i: (0, 0))],
        out_specs=out_specs,
        out_shape=out_shape,
        compiler_params=pltpu.CompilerParams(vmem_limit_bytes=56 * 1024 * 1024),
        name="prepare",
    )(x, ada, cos, sin, w_packed)


def _topk_threshold(sc_ref, nc, kcol, nvalid, row_max, row_min):
    rows = sc_ref.shape[1]

    def count_gt(tcol):
        tb = jnp.broadcast_to(tcol, (rows, CK))

        def body(c, acc):
            return acc + jnp.where(sc_ref[c] > tb, 1.0, 0.0)

        acc = lax.fori_loop(0, nc, body, jnp.zeros((rows, CK), F32))
        return jnp.sum(acc, axis=1, keepdims=True)

    lo0 = row_min - 1.0 - jnp.abs(row_min) * (2.0 ** -10)
    take_all = nvalid <= kcol
    zeros = jnp.zeros_like(row_max)
    active0 = jnp.where(take_all, 0.0, 1.0)
    init = (lo0, row_max, lo0, zeros, active0, zeros,
            jnp.sum(active0).astype(jnp.int32), jnp.int32(0))

    def cond(st):
        return jnp.logical_and(st[6] > 0, st[7] < 400)

    def body(st):
        lo, hi, t, c_hi, active, tie, _, it = st
        mid = lo + 0.5 * (hi - lo)
        c = count_gt(mid)
        on = active > 0.5
        hit = c == kcol
        spent = jnp.logical_or(mid <= lo, mid >= hi)
        go_up = jnp.logical_and(on, c > kcol)
        go_down = jnp.logical_and(on, c < kcol)
        finish_hit = jnp.logical_and(on, hit)
        finish_tie = jnp.logical_and(on, jnp.logical_and(spent, jnp.logical_not(hit)))
        t = jnp.where(finish_hit, mid, t)
        tie = jnp.where(finish_tie, 1.0, tie)
        move_lo = jnp.logical_and(go_up, jnp.logical_not(finish_tie))
        move_hi = jnp.logical_and(go_down, jnp.logical_not(finish_tie))
        lo = jnp.where(move_lo, mid, lo)
        c_hi = jnp.where(move_hi, c, c_hi)
        hi = jnp.where(move_hi, mid, hi)
        active = jnp.where(jnp.logical_or(finish_hit, finish_tie), 0.0, active)
        return (lo, hi, t, c_hi, active, tie, jnp.sum(active).astype(jnp.int32), it + 1)

    lo, hi, t, c_hi, _, tie, _, _ = lax.while_loop(cond, body, init)
    is_tie = tie > 0.5
    t = jnp.where(is_tie, lo, t)

    @pl.when(jnp.sum(tie) > 0.5)
    def _():
        need = kcol - c_hi
        ri = lax.broadcasted_iota(jnp.int32, (CK, CK), 0)
        ci = lax.broadcasted_iota(jnp.int32, (CK, CK), 1)
        tri = jnp.where(ri <= ci, 1.0, 0.0).astype(BF16)
        hib = jnp.broadcast_to(hi, (rows, CK))
        tieb = jnp.broadcast_to(tie, (rows, CK)) > 0.5

        def fix(c, before):
            s = sc_ref[c]
            eq = jnp.where(jnp.logical_and(s == hib, tieb), 1.0, 0.0)
            rank = before + _dot(eq.astype(BF16), tri)
            drop = jnp.logical_and(eq > 0.5, rank > need)
            sc_ref[c] = jnp.where(drop, -jnp.inf, s)
            return before + jnp.sum(eq, axis=1, keepdims=True)

        lax.fori_loop(0, nc, fix, jnp.zeros_like(hi))

    return t


def _online_softmax_step(h, s, vc, m_ref, l_ref, acc_ref):
    m_old = m_ref[h]
    m_new = jnp.maximum(m_old, jnp.max(s, axis=1, keepdims=True))
    p = jnp.exp(s - m_new)
    alpha = jnp.exp(m_old - m_new)
    l_ref[h] = alpha * l_ref[h] + jnp.sum(p, axis=1, keepdims=True)
    acc_ref[h] = alpha * acc_ref[h] + _dot(p.astype(BF16), vc)
    m_ref[h] = m_new


def _init_softmax_state(m_ref, l_ref, acc_ref):
    m_ref[...] = jnp.full(m_ref.shape, NEG, F32)
    l_ref[...] = jnp.zeros(l_ref.shape, F32)
    acc_ref[...] = jnp.zeros(acc_ref.shape, F32)


def _write_heads(o_ref, l_ref, acc_ref):
    lane = lax.broadcasted_iota(jnp.int32, (o_ref.shape[0], LANES), 1)
    is_low = lane < HEAD_DIM
    for pair in range(N_HEADS // 2):
        halves = []
        for h in (2 * pair, 2 * pair + 1):
            o = acc_ref[h] / l_ref[h]
            in_low = h < GROUP
            want_low = h % 2 == 0
            if in_low != want_low:
                o = pltpu.roll(o, HEAD_DIM, axis=1)
            halves.append(o)
        o_ref[:, pair * LANES:(pair + 1) * LANES] = jnp.where(is_low, halves[0], halves[1])


def _dsa_prompt_kernel(qi_ref, qa_ref, wt_ref, kiw_ref, ka_ref, va_ref, oa_ref,
                       sc_ref, wb_ref, m_ref, l_ref, acc_ref):
    qi = pl.program_id(1)
    nc = (qi * TQ + TQ + CK - 1) // CK
    qpos = qi * TQ + lax.broadcasted_iota(jnp.int32, (TQ, CK), 0)
    lane = lax.broadcasted_iota(jnp.int32, (TQ, CK), 1)

    wt = wt_ref[...]
    for h in range(N_HEADS):
        wb_ref[h] = jnp.broadcast_to(wt[:, IDX_DIM + h:IDX_DIM + h + 1], (TQ, CK))

    def score_chunk(c, carry):
        mx, mn = carry
        kc = kiw_ref[pl.ds(pl.multiple_of(c * CK, CK), CK), :]
        acc = jnp.zeros((TQ, CK), F32)
        for h in range(N_HEADS):
            acc = acc + wb_ref[h] * jnp.maximum(_dot_nt(qi_ref[h], kc), 0.0)
        valid = (c * CK + lane) <= qpos
        sc_ref[c] = jnp.where(valid, acc, -jnp.inf)
        mx = jnp.maximum(mx, jnp.where(valid, acc, -jnp.inf))
        mn = jnp.minimum(mn, jnp.where(valid, acc, jnp.inf))
        return mx, mn

    mx, mn = lax.fori_loop(0, nc, score_chunk,
                           (jnp.full((TQ, CK), -jnp.inf, F32), jnp.full((TQ, CK), jnp.inf, F32)))
    row_max = jnp.max(mx, axis=1, keepdims=True)
    row_min = jnp.min(mn, axis=1, keepdims=True)
    nvalid = (qpos[:, :1] + 1).astype(F32)
    kcol = jnp.minimum(nvalid, float(DSA_TOPK))
    t = _topk_threshold(sc_ref, nc, kcol, nvalid, row_max, row_min)

    _init_softmax_state(m_ref, l_ref, acc_ref)
    tb = jnp.broadcast_to(t, (TQ, CK))

    def attend_chunk(c, carry):
        off = pl.multiple_of(c * CK, CK)
        bias = jnp.where(sc_ref[c] > tb, 0.0, NEG)
        kc = ka_ref[pl.ds(off, CK), :]
        vc = va_ref[pl.ds(off, CK), :]
        for h in range(N_HEADS):
            _online_softmax_step(h, _dot_nt(qa_ref[h], kc) + bias, vc, m_ref, l_ref, acc_ref)
        return carry

    lax.fori_loop(0, nc, attend_chunk, 0)
    _write_heads(oa_ref, l_ref, acc_ref)


def _dsa_prompt(qi, qa, kiw32, kiw16, ka16, va16, n_batch, seq):
    nq = seq // TQ
    head_spec = pl.BlockSpec((N_HEADS, TQ, LANES), lambda n, i: (0, n * nq + i, 0))
    seq_spec = pl.BlockSpec((seq, LANES), lambda n, i: (n, 0))
    return pl.pallas_call(
        _dsa_prompt_kernel,
        grid=(n_batch, nq),
        in_specs=[head_spec, head_spec,
                  pl.BlockSpec((TQ, LANES), lambda n, i: (n * nq + i, 0)),
                  seq_spec, seq_spec, seq_spec],
        out_specs=pl.BlockSpec((TQ, BR_WIDTH), lambda n, i: (n * nq + i, 0)),
        out_shape=jax.ShapeDtypeStruct((n_batch * seq, BR_WIDTH), F32),
        scratch_shapes=[pltpu.VMEM((seq // CK, TQ, CK), F32),
                        pltpu.VMEM((N_HEADS, TQ, CK), F32),
                        pltpu.VMEM((N_HEADS, TQ, 1), F32),
                        pltpu.VMEM((N_HEADS, TQ, 1), F32),
                        pltpu.VMEM((N_HEADS, TQ, LANES), F32)],
        compiler_params=pltpu.CompilerParams(vmem_limit_bytes=48 * 1024 * 1024),
        name="dsa_prompt",
    )(qi, qa, kiw32, kiw16, ka16, va16)


def _moba_select(gate, n_past):
    blk = lax.broadcasted_iota(jnp.int32, gate.shape, 1)
    blkf = blk.astype(F32)
    past = blk < n_past
    g = jnp.where(past, gate, -jnp.inf)
    sel = jnp.zeros(gate.shape, F32)
    for _ in range(MOBA_TOPK):
        top = jnp.max(g, axis=1, keepdims=True)
        first = jnp.min(jnp.where(g == top, blkf, float(LANES)), axis=1, keepdims=True)
        pick = blkf == first
        sel = jnp.where(pick, 1.0, sel)
        g = jnp.where(pick, -jnp.inf, g)
    return jnp.where(past, sel, 0.0)


def _moba_prompt_kernel(qb_ref, kb_ref, vb_ref, km_ref, ob_ref, sel_ref, m_ref, l_ref, acc_ref):
    qi = pl.program_id(1)
    own = (qi * TQ) // MOBA_BLOCK
    km = km_ref[...].astype(BF16)
    for h in range(N_HEADS):
        sel_ref[h] = _moba_select(_dot_nt(qb_ref[h], km), own)

    _init_softmax_state(m_ref, l_ref, acc_ref)
    qpos = qi * TQ + lax.broadcasted_iota(jnp.int32, (TQ, CK), 0)
    lane = lax.broadcasted_iota(jnp.int32, (TQ, CK), 1)
    blk_row = lax.broadcasted_iota(jnp.int32, (LANES, CK), 0)

    def attend_block(b, carry):
        off = pl.multiple_of(b * CK, CK)
        kc = kb_ref[pl.ds(off, CK), :]
        vc = vb_ref[pl.ds(off, CK), :]
        expand = jnp.where(blk_row == b, 1.0, 0.0).astype(BF16)
        causal_bias = jnp.where((b * CK + lane) <= qpos, 0.0, NEG)
        is_own = b == own
        for h in range(N_HEADS):
            chosen = _dot(sel_ref[h].astype(BF16), expand)
            bias = jnp.where(is_own, causal_bias, (chosen - 1.0) * (-NEG))
            _online_softmax_step(h, _dot_nt(qb_ref[h], kc) + bias, vc, m_ref, l_ref, acc_ref)
        return carry

    lax.fori_loop(0, own + 1, attend_block, 0)
    _write_heads(ob_ref, l_ref, acc_ref)


def _moba_prompt(qb, kb16, vb16, kmean, n_batch, seq):
    nq = seq // TQ
    head_spec = pl.BlockSpec((N_HEADS, TQ, LANES), lambda n, i: (0, n * nq + i, 0))
    seq_spec = pl.BlockSpec((seq, LANES), lambda n, i: (n, 0))
    return pl.pallas_call(
        _moba_prompt_kernel,
        grid=(n_batch, nq),
        in_specs=[head_spec, seq_spec, seq_spec,
                  pl.BlockSpec((None, LANES, LANES), lambda n, i: (n, 0, 0))],
        out_specs=pl.BlockSpec((TQ, BR_WIDTH), lambda n, i: (n * nq + i, 0)),
        out_shape=jax.ShapeDtypeStruct((n_batch * seq, BR_WIDTH), F32),
        scratch_shapes=[pltpu.VMEM((N_HEADS, TQ, LANES), F32),
                        pltpu.VMEM((N_HEADS, TQ, 1), F32),
                        pltpu.VMEM((N_HEADS, TQ, 1), F32),
                        pltpu.VMEM((N_HEADS, TQ, LANES), F32)],
        compiler_params=pltpu.CompilerParams(vmem_limit_bytes=40 * 1024 * 1024),
        name="moba_prompt",
    )(qb, kb16, vb16, kmean)


def _page_copy(cache_ref, buf_ref, sem_ref, page, slot, p):
    window = pl.ds(pl.multiple_of(p * PAGE, PAGE), PAGE)
    return pltpu.make_async_copy(cache_ref.at[page], buf_ref.at[slot, :, window], sem_ref.at[slot])


def _start_pages(pt_ref, seq_idx, slot, n_pages, streams):
    def body(p, carry):
        page = pt_ref[seq_idx, p]
        for cache_ref, buf_ref, sem_ref in streams:
            _page_copy(cache_ref, buf_ref, sem_ref, page, slot, p).start()
        return carry

    lax.fori_loop(0, n_pages, body, 0)


def _wait_pages(slot, n_pages, streams):
    def body(p, carry):
        for cache_ref, buf_ref, sem_ref in streams:
            _page_copy(cache_ref, buf_ref, sem_ref, 0, slot, p).wait()
        return carry

    lax.fori_loop(0, n_pages, body, 0)


def _gather_pipeline(pt_ref, n_pages, streams):
    s = pl.program_id(0)
    slot = s % 2

    @pl.when(s == 0)
    def _():
        _start_pages(pt_ref, s, slot, n_pages, streams)

    @pl.when(s + 1 < pl.num_programs(0))
    def _():
        _start_pages(pt_ref, s + 1, 1 - slot, n_pages, streams)

    _wait_pages(slot, n_pages, streams)
    return slot


def _dec_scores_kernel(pt_ref, q_ref, w_ref, knew_ref, cache_ref, o_ref, kbuf, sem):
    n_pages = pt_ref.shape[1]
    past = n_pages * PAGE
    slot = _gather_pipeline(pt_ref, n_pages, [(cache_ref, kbuf, sem)])
    q = q_ref[...]
    w = w_ref[...]
    qk = q[:, :IDX_DIM]
    for ch in range(past // DEC_CHUNK):
        kc = kbuf[slot, :, ch * DEC_CHUNK:(ch + 1) * DEC_CHUNK].astype(BF16)
        lg = _dot(qk, kc)
        o_ref[:, ch * DEC_CHUNK:(ch + 1) * DEC_CHUNK] = jnp.sum(
            w * jnp.maximum(lg, 0.0), axis=0, keepdims=True)
    knew = knew_ref[...].astype(BF16).astype(F32)
    lg_new = jnp.sum(q.astype(F32) * knew, axis=1, keepdims=True)
    s_new = jnp.sum(w * jnp.maximum(lg_new, 0.0), axis=0, keepdims=True)
    lane = lax.broadcasted_iota(jnp.int32, (1, CK), 1)
    o_ref[:, past:past + CK] = jnp.where(lane == 0, s_new, -jnp.inf)


def _dec_scores(page_table, qi_s, w_s, knew, cache):
    ns, n_pages = page_table.shape
    past = n_pages * PAGE
    width = past + CK
    grid_spec = pltpu.PrefetchScalarGridSpec(
        num_scalar_prefetch=1,
        grid=(ns,),
        in_specs=[pl.BlockSpec((None, N_HEADS, LANES), lambda s, pt: (s, 0, 0)),
                  pl.BlockSpec((None, N_HEADS, 1), lambda s, pt: (s, 0, 0)),
                  pl.BlockSpec((None, 1, LANES), lambda s, pt: (s, 0, 0)),
                  pl.BlockSpec(memory_space=pl.ANY)],
        out_specs=pl.BlockSpec((None, 1, width), lambda s, pt: (s, 0, 0)),
        scratch_shapes=[pltpu.VMEM((2, IDX_DIM, past), F32),
                        pltpu.SemaphoreType.DMA((2,))])
    return pl.pallas_call(
        _dec_scores_kernel,
        grid_spec=grid_spec,
        out_shape=jax.ShapeDtypeStruct((ns, 1, width), F32),
        compiler_params=pltpu.CompilerParams(dimension_semantics=("arbitrary",),
                                             vmem_limit_bytes=32 * 1024 * 1024),
        name="dec_scores",
    )(page_table, qi_s, w_s, knew, cache)


def _dec_select_kernel(s_ref, bias_ref, sc_ref):
    rows, width = s_ref.shape
    nch = width // CK
    mx = jnp.full((rows, CK), -jnp.inf, F32)
    mn = jnp.full((rows, CK), jnp.inf, F32)
    for c in range(nch):
        s = s_ref[:, c * CK:(c + 1) * CK]
        sc_ref[c] = s
        mx = jnp.maximum(mx, s)
        mn = jnp.minimum(mn, jnp.where(s > -jnp.inf, s, jnp.inf))
    row_max = jnp.max(mx, axis=1, keepdims=True)
    row_min = jnp.min(mn, axis=1, keepdims=True)
    nvalid = jnp.full((rows, 1), float(width - CK + 1), F32)
    kcol = jnp.minimum(nvalid, float(DSA_TOPK))
    t = _topk_threshold(sc_ref, nch, kcol, nvalid, row_max, row_min)
    tb = jnp.broadcast_to(t, (rows, CK))
    for c in range(nch):
        bias_ref[:, c * CK:(c + 1) * CK] = jnp.where(sc_ref[c] > tb, 0.0, NEG)


def _dec_select(scores):
    rows, width = scores.shape
    return pl.pallas_call(
        _dec_select_kernel,
        grid=(1,),
        in_specs=[pl.BlockSpec((rows, width), lambda i: (0, 0))],
        out_specs=pl.BlockSpec((rows, width), lambda i: (0, 0)),
        out_shape=jax.ShapeDtypeStruct((rows, width), F32),
        scratch_shapes=[pltpu.VMEM((width // CK, rows, CK), F32)],
        compiler_params=pltpu.CompilerParams(vmem_limit_bytes=40 * 1024 * 1024),
        name="dec_select",
    )(scores)


def _decode_scores(q, kbuf, slot, past, s_ref):
    for ch in range(past // DEC_CHUNK):
        win = slice(ch * DEC_CHUNK, (ch + 1) * DEC_CHUNK)
        s_ref[:, win] = _dot(q, kbuf[slot, :, win].astype(BF16))


def _decode_softmax_pv(q, vbuf, slot, past, s_ref, knew, vnew, bias_new, o_ref):
    n_ch = past // DEC_CHUNK
    s_new = jnp.sum(q.astype(F32) * knew.astype(BF16).astype(F32), axis=1, keepdims=True) + bias_new
    m = s_new
    for ch in range(n_ch):
        win = slice(ch * DEC_CHUNK, (ch + 1) * DEC_CHUNK)
        m = jnp.maximum(m, jnp.max(s_ref[:, win], axis=1, keepdims=True))
    p_new = jnp.exp(s_new - m)
    l = p_new
    acc = p_new * vnew.astype(BF16).astype(F32)
    for ch in range(n_ch):
        win = slice(ch * DEC_CHUNK, (ch + 1) * DEC_CHUNK)
        p = jnp.exp(s_ref[:, win] - m)
        l = l + jnp.sum(p, axis=1, keepdims=True)
        acc = acc + _dot_nt(p.astype(BF16), vbuf[slot, :, win].astype(BF16))
    o = acc / l
    head = lax.broadcasted_iota(jnp.int32, o.shape, 0)
    o_ref[...] = jnp.where(head < GROUP, o, pltpu.roll(o, HEAD_DIM, axis=1))


def _dec_dsa_kernel(pt_ref, q_ref, bias_ref, knew_ref, vnew_ref, kc_ref, vc_ref, o_ref,
                    kbuf, vbuf, s_ref, ksem, vsem):
    n_pages = pt_ref.shape[1]
    past = n_pages * PAGE
    slot = _gather_pipeline(pt_ref, n_pages, [(kc_ref, kbuf, ksem), (vc_ref, vbuf, vsem)])
    q = q_ref[...]
    _decode_scores(q, kbuf, slot, past, s_ref)
    for ch in range(past // DEC_CHUNK):
        win = slice(ch * DEC_CHUNK, (ch + 1) * DEC_CHUNK)
        s_ref[:, win] = s_ref[:, win] + bias_ref[:, win]
    _decode_softmax_pv(q, vbuf, slot, past, s_ref, knew_ref[...], vnew_ref[...],
                       bias_ref[:, past:past + 1], o_ref)


def _dec_moba_kernel(pt_ref, q_ref, knew_ref, vnew_ref, kc_ref, vc_ref, o_ref,
                     kbuf, vbuf, s_ref, ksem, vsem):
    n_pages = pt_ref.shape[1]
    past = n_pages * PAGE
    n_blocks = past // MOBA_BLOCK
    slot = _gather_pipeline(pt_ref, n_pages, [(kc_ref, kbuf, ksem), (vc_ref, vbuf, vsem)])
    q = q_ref[...]
    _decode_scores(q, kbuf, slot, past, s_ref)
    lane = lax.broadcasted_iota(jnp.int32, (N_HEADS, LANES), 1)
    gate = jnp.zeros((N_HEADS, LANES), F32)
    for b in range(n_blocks):
        blk_sum = jnp.sum(s_ref[:, b * MOBA_BLOCK:(b + 1) * MOBA_BLOCK], axis=1, keepdims=True)
        gate = jnp.where(lane == b, blk_sum * (1.0 / MOBA_BLOCK), gate)
    sel = _moba_select(gate, n_blocks)
    for b in range(n_blocks):
        win = slice(b * MOBA_BLOCK, (b + 1) * MOBA_BLOCK)
        s_ref[:, win] = s_ref[:, win] + (sel[:, b:b + 1] - 1.0) * (-NEG)
    _decode_softmax_pv(q, vbuf, slot, past, s_ref, knew_ref[...], vnew_ref[...], 0.0, o_ref)


def _dec_attend_call(kernel, name, page_table, q_s, extra, knew, vnew, k_cache, v_cache):
    ns, n_pages = page_table.shape
    past = n_pages * PAGE
    per_seq = lambda shape: pl.BlockSpec((None,) + shape, lambda s, pt: (s, 0, 0))
    in_specs = [per_seq((N_HEADS, LANES))]
    in_specs += [per_seq((1, e.shape[2])) for e in extra]
    in_specs += [per_seq((1, LANES)), per_seq((1, LANES)),
                 pl.BlockSpec(memory_space=pl.ANY), pl.BlockSpec(memory_space=pl.ANY)]
    grid_spec = pltpu.PrefetchScalarGridSpec(
        num_scalar_prefetch=1,
        grid=(ns,),
        in_specs=in_specs,
        out_specs=per_seq((N_HEADS, LANES)),
        scratch_shapes=[pltpu.VMEM((2, LANES, past), F32),
                        pltpu.VMEM((2, LANES, past), F32),
                        pltpu.VMEM((N_HEADS, past), F32),
                        pltpu.SemaphoreType.DMA((2,)), pltpu.SemaphoreType.DMA((2,))])
    return pl.pallas_call(
        kernel,
        grid_spec=grid_spec,
        out_shape=jax.ShapeDtypeStruct((ns, N_HEADS, LANES), F32),
        compiler_params=pltpu.CompilerParams(dimension_semantics=("arbitrary",),
                                             vmem_limit_bytes=48 * 1024 * 1024),
        name=name,
    )(page_table, q_s, *extra, knew, vnew, k_cache, v_cache)


def _heads_from_padded(o):
    return o[:, :, :HEAD_DIM].reshape(o.shape[0], BR_WIDTH)


def _finish_kernel(x_ref, oa_ref, ob_ref, za_ref, zb_ref, ga_ref, gb_ref, ada_ref,
                   wa_ref, wb_ref, wo_ref, gf_ref, y_ref):
    ya = _dot((oa_ref[...] * za_ref[...].astype(F32)).astype(BF16), wa_ref[...])
    yb = _dot((ob_ref[...] * zb_ref[...].astype(F32)).astype(BF16), wb_ref[...])
    m = ga_ref[...].astype(F32) * ya + gb_ref[...].astype(F32) * yb
    gate = ada_ref[...][:, 2 * D_MODEL:]
    xo = x_ref[...] + gate * _dot(m.astype(BF16), wo_ref[...])
    ms = jnp.mean(xo * xo, axis=-1, keepdims=True)
    y_ref[...] = (xo * lax.rsqrt(ms + EPS)) * gf_ref[...]


def _finish(x, oa, ob, za, zb, ga, gb, ada, wa, wb, wo, gf, tm, rows_per_ada):
    rows = x.shape[0]
    tiles_per_ada = rows_per_ada // tm
    a_rows = ada.shape[1]
    row_spec = lambda w: pl.BlockSpec((tm, w), lambda i: (i, 0))
    full = lambda a: pl.BlockSpec(a.shape, lambda i: (0, 0))
    return pl.pallas_call(
        _finish_kernel,
        grid=(rows // tm,),
        in_specs=[row_spec(D_MODEL), row_spec(BR_WIDTH), row_spec(BR_WIDTH),
                  row_spec(BR_WIDTH), row_spec(BR_WIDTH), row_spec(D_MODEL), row_spec(D_MODEL),
                  pl.BlockSpec((None, a_rows, 3 * D_MODEL), lambda i: (i // tiles_per_ada, 0, 0)),
                  full(wa), full(wb), full(wo), full(gf)],
        out_specs=row_spec(D_MODEL),
        out_shape=jax.ShapeDtypeStruct((rows, D_MODEL), F32),
        compiler_params=pltpu.CompilerParams(vmem_limit_bytes=40 * 1024 * 1024),
        name="finish",
    )(x, oa, ob, za, zb, ga, gb, ada, wa, wb, wo, gf)


def _rope_tables(pos):
    half = HEAD_DIM // 2
    freqs = ROPE_THETA ** (-jnp.arange(half, dtype=F32) / half)
    ang = pos.astype(F32)[:, None] * freqs[None, :]
    cos = jnp.cos(ang)
    sin = jnp.sin(ang)
    cos_t = jnp.concatenate([cos, cos, cos, cos], axis=1)
    sin_t = jnp.concatenate([-sin, sin, -sin, sin], axis=1)
    return cos_t, sin_t


def kernel(x_prompt, x_sample, cache_k_a, cache_v_a, cache_k_idx, cache_k_b, cache_v_b, page_table,
           c_prompt, c_sample, w_ada, b_ada, w_in, w_br_a, w_br_b, w_o, g_final):
    n_batch, seq, _ = x_prompt.shape
    ns, dec_seq, _ = x_sample.shape
    depth = w_in.shape[0]
    assert depth == 1 and dec_seq == 1 and seq % MOBA_BLOCK == 0
    n_pages = page_table.shape[1]
    past = n_pages * PAGE
    n_pool = cache_k_a.shape[1]
    tm = MOBA_BLOCK

    pad_rows = (-(n_batch + ns)) % 8
    c_all = jnp.concatenate([c_prompt, c_sample, jnp.zeros((pad_rows, D_MODEL), F32)], axis=0)
    ada = _ada(c_all, w_ada[0], b_ada[0][None, :])
    ada_p = ada[:n_batch].reshape(n_batch, 1, 3 * D_MODEL)
    ada_s = ada[n_batch:n_batch + ns].reshape(1, ns, 3 * D_MODEL)

    w_packed = jnp.concatenate(
        [w_in[0][:, :_ORIG_SPLIT], jnp.zeros((D_MODEL, _PAD_COLS), F32), w_in[0][:, _ORIG_SPLIT:]],
        axis=1).astype(BF16)
    cos_p, sin_p = _rope_tables(jnp.arange(seq, dtype=jnp.int32))
    cos_s, sin_s = _rope_tables(jnp.full((ns,), past, dtype=jnp.int32))

    (qa, qi, qb, ka32, va32, kiw32, kb32, vb32, ka16, va16, kiw16, kb16, vb16,
     za, zb, ga, gb, kmean) = _prepare(x_prompt.reshape(n_batch * seq, D_MODEL), ada_p,
                                       cos_p, sin_p, w_packed, tm, seq)
    oa = _dsa_prompt(qi, qa, kiw32, kiw16, ka16, va16, n_batch, seq)
    n_blk = seq // MOBA_BLOCK
    km_pad = jnp.pad(kmean.reshape(n_batch, n_blk, LANES), ((0, 0), (0, LANES - n_blk), (0, 0)))
    ob = _moba_prompt(qb, kb16, vb16, km_pad, n_batch, seq)
    wa = w_br_a[0].astype(BF16)
    wb = w_br_b[0].astype(BF16)
    wo = w_o[0].astype(BF16)
    gf = g_final[None, :]
    y_p = _finish(x_prompt.reshape(n_batch * seq, D_MODEL), oa, ob, za, zb, ga, gb, ada_p,
                  wa, wb, wo, gf, tm, seq)

    (qa_s, qi_s, qb_s, ka32_s, va32_s, kiw32_s, kb32_s, vb32_s, _, _, _, _, _,
     za_s, zb_s, ga_s, gb_s, _) = _prepare(x_sample.reshape(ns, D_MODEL), ada_s,
                                           cos_s, sin_s, w_packed, ns, ns)
    per_seq_heads = lambda q: jnp.transpose(q, (1, 0, 2))
    row3 = lambda a: a.reshape(ns, 1, LANES)
    w_s = kiw32_s[:, IDX_DIM:IDX_DIM + N_HEADS].reshape(ns, N_HEADS, 1)
    kv_pages = lambda c: jnp.transpose(c[0], (0, 2, 3, 1)).reshape(n_pool, LANES, PAGE)
    idx_pages = jnp.transpose(cache_k_idx[0], (0, 2, 1))
    scores = _dec_scores(page_table, per_seq_heads(qi_s), w_s, row3(kiw32_s), idx_pages)
    bias = _dec_select(scores.reshape(ns, past + CK)).reshape(ns, 1, past + CK)
    oa_s = _dec_attend_call(_dec_dsa_kernel, "dec_dsa", page_table, per_seq_heads(qa_s), [bias],
                            row3(ka32_s), row3(va32_s), kv_pages(cache_k_a), kv_pages(cache_v_a))
    ob_s = _dec_attend_call(_dec_moba_kernel, "dec_moba", page_table, per_seq_heads(qb_s), [],
                            row3(kb32_s), row3(vb32_s), kv_pages(cache_k_b), kv_pages(cache_v_b))
    y_s = _finish(x_sample.reshape(ns, D_MODEL), _heads_from_padded(oa_s), _heads_from_padded(ob_s),
                  za_s, zb_s, ga_s, gb_s, ada_s, wa, wb, wo, gf, ns, ns)

    kv = lambda a, n, t: a.reshape(1, n, t, KV_HEADS, HEAD_DIM)
    ki = lambda a, n, t: a[:, :IDX_DIM].reshape(1, n, t, IDX_DIM)
    return (y_p.reshape(n_batch, seq, D_MODEL), y_s.reshape(ns, 1, D_MODEL),
            kv(ka32, n_batch, seq), kv(va32, n_batch, seq), ki(kiw32, n_batch, seq),
            kv(kb32, n_batch, seq), kv(vb32, n_batch, seq),
            kv(ka32_s, ns, 1), kv(va32_s, ns, 1), ki(kiw32_s, ns, 1),
            kv(kb32_s, ns, 1), kv(vb32_s, ns, 1))
```

```python
import functools

import jax
import jax.numpy as jnp
from jax import lax
from jax.experimental import pallas as pl
from jax.experimental.pallas import tpu as pltpu

F32 = jnp.float32
BF16 = jnp.bfloat16

D_MODEL = 1024
HEAD_DIM = 64
N_HEADS = 8
KV_HEADS = 2
GROUP = N_HEADS // KV_HEADS
IDX_DIM = 64
BR_WIDTH = N_HEADS * HEAD_DIM
DSA_TOPK = 256
MOBA_BLOCK = 256
MOBA_TOPK = 3
PAGE = 128
ROPE_THETA = 10000.0
EPS = 1e-6
LANES = 128
CK = 256
TQ = 128
NEG = -1e30
DEC_CHUNK = 1024

_ORIG_SPLIT = 1864
_PAD_COLS = 56
C_QA, C_KA, C_VA, C_ZA, C_QI, C_KIW, C_QB, C_KB, C_VB, C_ZB, C_GA, C_GB, C_END = (
    0, 512, 640, 768, 1280, 1792, 1920, 2432, 2560, 2688, 3200, 4224, 5248)

LOG2E = 1.4426950408889634
_ATT_WIDTHS = (LANES, 2 * LANES, LANES, 2 * LANES, 2 * LANES)

_NT = (((1,), (1,)), ((), ()))


def _dot_nt(a, b):
    return lax.dot_general(a, b, _NT, preferred_element_type=F32)


def _dot(a, b):
    return jnp.dot(a, b, preferred_element_type=F32)


def _ada_kernel(c_ref, w_ref, b_ref, o_ref):
    o_ref[...] = _dot(c_ref[...].astype(BF16), w_ref[...].astype(BF16)) + b_ref[...]


def _ada(c_all, w_ada, b_ada):
    rows = c_all.shape[0]
    return pl.pallas_call(
        _ada_kernel,
        grid=(3,),
        in_specs=[pl.BlockSpec((rows, D_MODEL), lambda j: (0, 0)),
                  pl.BlockSpec((D_MODEL, D_MODEL), lambda j: (0, j)),
                  pl.BlockSpec((1, D_MODEL), lambda j: (0, j))],
        out_specs=pl.BlockSpec((rows, D_MODEL), lambda j: (0, j)),
        out_shape=jax.ShapeDtypeStruct((rows, 3 * D_MODEL), F32),
        name="ada",
    )(c_all, w_ada, b_ada)


def _rope(x, cos, sin_signed):
    w = x.shape[1]
    reps = w // LANES
    lane = lax.broadcasted_iota(jnp.int32, x.shape, 1)
    upper = (lane % HEAD_DIM) >= (HEAD_DIM // 2)
    from_below = pltpu.roll(x, HEAD_DIM // 2, axis=1)
    from_above = pltpu.roll(x, w - HEAD_DIM // 2, axis=1)
    partner = jnp.where(upper, from_below, from_above)
    if reps > 1:
        cos = jnp.concatenate([cos] * reps, axis=1)
        sin_signed = jnp.concatenate([sin_signed] * reps, axis=1)
    return x * cos + partner * sin_signed


def _store_padded_heads(q, out_ref, low_for_head):
    lane = lax.broadcasted_iota(jnp.int32, (q.shape[0], LANES), 1)
    is_low = lane < HEAD_DIM
    for h in range(N_HEADS):
        col = q[:, (h // 2) * LANES:(h // 2 + 1) * LANES]
        want_low = low_for_head(h)
        if (h % 2 == 0) != want_low:
            col = pltpu.roll(col, HEAD_DIM, axis=1)
        keep = is_low if want_low else jnp.logical_not(is_low)
        out_ref[h] = jnp.where(keep, col, 0.0).astype(BF16)


def _prep_kernel(x_ref, ada_ref, cos_ref, sin_ref, w_ref,
                 qa_ref, qi_ref, qb_ref,
                 ka32_ref, va32_ref, kiw32_ref, kb32_ref, vb32_ref,
                 ka16_ref, va16_ref, kiw16_ref, kb16_ref, vb16_ref,
                 za_ref, zb_ref, ga_ref, gb_ref, km_ref, *, tiles_per_seq):
    x = x_ref[...]
    ada = ada_ref[...]
    shift = ada[:, :D_MODEL]
    scale = ada[:, D_MODEL:2 * D_MODEL]
    ms = jnp.mean(x * x, axis=-1, keepdims=True)
    h = (x * lax.rsqrt(ms + EPS)) * (1.0 + scale) + shift
    hb = h.astype(BF16)
    cos = cos_ref[...]
    sin = sin_ref[...]

    def proj(a, b):
        return _dot(hb, w_ref[:, a:b])

    kv_low = lambda hh: hh < GROUP
    qscale = HEAD_DIM ** -0.5 * LOG2E
    _store_padded_heads(_rope(proj(C_QA, C_KA), cos, sin) * qscale, qa_ref, kv_low)
    _store_padded_heads(_rope(proj(C_QI, C_KIW), cos, sin) * (IDX_DIM ** -0.5), qi_ref,
                        lambda hh: True)
    _store_padded_heads(_rope(proj(C_QB, C_KB), cos, sin) * qscale, qb_ref, kv_low)

    ka = _rope(proj(C_KA, C_VA), cos, sin)
    ka32_ref[...] = ka
    ka16_ref[...] = ka.astype(BF16)
    va = proj(C_VA, C_ZA)
    va32_ref[...] = va
    ones = jnp.ones(va.shape, BF16)
    va16_ref[...] = jnp.concatenate([va.astype(BF16), ones], axis=1)

    kiw = proj(C_KIW, C_QB)
    lane = lax.broadcasted_iota(jnp.int32, kiw.shape, 1)
    kiw = jnp.where(lane < IDX_DIM, _rope(kiw, cos, sin), kiw * (N_HEADS ** -0.5))
    kiw32_ref[...] = kiw
    kiw16_ref[...] = kiw.astype(BF16)

    kb = _rope(proj(C_KB, C_VB), cos, sin)
    kb32_ref[...] = kb
    own_block = pl.program_id(0) % tiles_per_seq
    block_code = jnp.where(lane == own_block, 1.0, 0.0).astype(BF16)
    kb16_ref[...] = jnp.concatenate([kb.astype(BF16), block_code], axis=1)
    km_ref[0] = jnp.sum(kb, axis=0, keepdims=True) * (1.0 / MOBA_BLOCK)
    vb = proj(C_VB, C_ZB)
    vb32_ref[...] = vb
    vb16_ref[...] = jnp.concatenate([vb.astype(BF16), ones], axis=1)

    za_ref[...] = jax.nn.silu(proj(C_ZA, C_QI)).astype(BF16)
    zb_ref[...] = jax.nn.silu(proj(C_ZB, C_GA)).astype(BF16)
    ga_ref[...] = jax.nn.sigmoid(proj(C_GA, C_GB)).astype(BF16)
    gb_ref[...] = jax.nn.sigmoid(proj(C_GB, C_END)).astype(BF16)


def _prepare(x, ada, cos, sin, w_packed, tm, rows_per_ada):
    rows = x.shape[0]
    nt = rows // tm
    tab_tiles = cos.shape[0] // tm
    tiles_per_ada = rows_per_ada // tm
    a_rows = ada.shape[1]
    row_spec = lambda w: pl.BlockSpec((tm, w), lambda i: (i, 0))
    head_spec = pl.BlockSpec((N_HEADS, tm, LANES), lambda i: (0, i, 0))
    sd = jax.ShapeDtypeStruct
    out_shape = (
        [sd((N_HEADS, rows, LANES), BF16)] * 3
        + [sd((rows, LANES), F32)] * 5
        + [sd((rows, w), BF16) for w in _ATT_WIDTHS]
        + [sd((rows, BR_WIDTH), BF16)] * 2
        + [sd((rows, D_MODEL), BF16)] * 2
        + [sd((nt, 1, LANES), F32)])
    out_specs = (
        [head_spec] * 3 + [row_spec(LANES)] * 5 + [row_spec(w) for w in _ATT_WIDTHS]
        + [row_spec(BR_WIDTH)] * 2
        + [row_spec(D_MODEL)] * 2 + [pl.BlockSpec((1, 1, LANES), lambda i: (i, 0, 0))])
    return pl.pallas_call(
        functools.partial(_prep_kernel, tiles_per_seq=tiles_per_ada),
        grid=(nt,),
        in_specs=[row_spec(D_MODEL),
                  pl.BlockSpec((None, a_rows, 3 * D_MODEL), lambda i: (i // tiles_per_ada, 0, 0)),
                  pl.BlockSpec((tm, LANES), lambda i: (i % tab_tiles, 0)),
                  pl.BlockSpec((tm, LANES), lambda i: (i % tab_tiles, 0)),
                  pl.BlockSpec((D_MODEL, C_END), lambda i: (0, 0))],
        out_specs=out_specs,
        out_shape=out_shape,
        compiler_params=pltpu.CompilerParams(vmem_limit_bytes=56 * 1024 * 1024),
        name="prepare",
    )(x, ada, cos, sin, w_packed)


def _topk_threshold(sc_ref, nc, kcol, nvalid, row_max, row_min):
    rows = sc_ref.shape[1]

    def count_gt(tcol):
        tb = jnp.broadcast_to(tcol, (rows, CK))

        def body(c, acc):
            return acc + jnp.where(sc_ref[c] > tb, 1.0, 0.0)

        acc = lax.fori_loop(0, nc, body, jnp.zeros((rows, CK), F32))
        return jnp.sum(acc, axis=1, keepdims=True)

    lo0 = row_min - 1.0 - jnp.abs(row_min) * (2.0 ** -10)
    take_all = nvalid <= kcol
    zeros = jnp.zeros_like(row_max)
    active0 = jnp.where(take_all, 0.0, 1.0)
    init = (lo0, row_max, lo0, zeros, active0, zeros,
            jnp.sum(active0).astype(jnp.int32), jnp.int32(0))

    def cond(st):
        return jnp.logical_and(st[6] > 0, st[7] < 400)

    def body(st):
        lo, hi, t, c_hi, active, tie, _, it = st
        mid = lo + 0.5 * (hi - lo)
        c = count_gt(mid)
        on = active > 0.5
        hit = c == kcol
        spent = jnp.logical_or(mid <= lo, mid >= hi)
        go_up = jnp.logical_and(on, c > kcol)
        go_down = jnp.logical_and(on, c < kcol)
        finish_hit = jnp.logical_and(on, hit)
        finish_tie = jnp.logical_and(on, jnp.logical_and(spent, jnp.logical_not(hit)))
        t = jnp.where(finish_hit, mid, t)
        tie = jnp.where(finish_tie, 1.0, tie)
        move_lo = jnp.logical_and(go_up, jnp.logical_not(finish_tie))
        move_hi = jnp.logical_and(go_down, jnp.logical_not(finish_tie))
        lo = jnp.where(move_lo, mid, lo)
        c_hi = jnp.where(move_hi, c, c_hi)
        hi = jnp.where(move_hi, mid, hi)
        active = jnp.where(jnp.logical_or(finish_hit, finish_tie), 0.0, active)
        return (lo, hi, t, c_hi, active, tie, jnp.sum(active).astype(jnp.int32), it + 1)

    lo, hi, t, c_hi, _, tie, _, _ = lax.while_loop(cond, body, init)
    is_tie = tie > 0.5
    t = jnp.where(is_tie, lo, t)

    @pl.when(jnp.sum(tie) > 0.5)
    def _():
        need = kcol - c_hi
        ri = lax.broadcasted_iota(jnp.int32, (CK, CK), 0)
        ci = lax.broadcasted_iota(jnp.int32, (CK, CK), 1)
        tri = jnp.where(ri <= ci, 1.0, 0.0).astype(BF16)
        hib = jnp.broadcast_to(hi, (rows, CK))
        tieb = jnp.broadcast_to(tie, (rows, CK)) > 0.5

        def fix(c, before):
            s = sc_ref[c]
            eq = jnp.where(jnp.logical_and(s == hib, tieb), 1.0, 0.0)
            rank = before + _dot(eq.astype(BF16), tri)
            drop = jnp.logical_and(eq > 0.5, rank > need)
            sc_ref[c] = jnp.where(drop, -jnp.inf, s)
            return before + jnp.sum(eq, axis=1, keepdims=True)

        lax.fori_loop(0, nc, fix, jnp.zeros_like(hi))

    return t


def _scores(q_ref, kc):
    q = q_ref[...].reshape(N_HEADS * TQ, q_ref.shape[2])
    return _dot_nt(q, kc).reshape(N_HEADS, TQ, CK)


def _track_max(s, mrun_ref):
    mrun_ref[...] = jnp.maximum(mrun_ref[...], jnp.maximum(s[:, :, :LANES], s[:, :, LANES:]))


def _finish_max(mrun_ref, mb_ref):
    m = jnp.max(mrun_ref[...], axis=2, keepdims=True)
    mb_ref[...] = jnp.broadcast_to(m, mb_ref.shape)


def _accumulate_pv(s, vc, mb_ref, acc_ref):
    p = jnp.exp2(s - mb_ref[...]).astype(BF16).reshape(N_HEADS * TQ, CK)
    acc_ref[...] = acc_ref[...] + _dot(p, vc).reshape(acc_ref.shape)


def _two_pass_attention(n_chunks, q_ref, k_ref, v_ref, bias_of_chunk, mrun_ref, mb_ref, acc_ref):
    mrun_ref[...] = jnp.full(mrun_ref.shape, NEG, F32)
    acc_ref[...] = jnp.zeros(acc_ref.shape, F32)
    n_pairs = (n_chunks + 1) // 2

    def max_pair(i, carry):
        for j in range(2):
            c = 2 * i + j
            kc = k_ref[pl.ds(pl.multiple_of(c * CK, CK), CK), :]
            _track_max(_scores(q_ref, kc) + bias_of_chunk(c)[None], mrun_ref)
        return carry

    lax.fori_loop(0, n_pairs, max_pair, 0)
    _finish_max(mrun_ref, mb_ref)

    def attend_pair(i, carry):
        for j in range(2):
            c = 2 * i + j
            off = pl.multiple_of(c * CK, CK)
            _accumulate_pv(_scores(q_ref, k_ref[pl.ds(off, CK), :]) + bias_of_chunk(c)[None],
                           v_ref[pl.ds(off, CK), :], mb_ref, acc_ref)
        return carry

    lax.fori_loop(0, n_pairs, attend_pair, 0)


def _write_heads(o_ref, acc_ref):
    lane = lax.broadcasted_iota(jnp.int32, (o_ref.shape[0], LANES), 1)
    is_low = lane < HEAD_DIM
    for pair in range(N_HEADS // 2):
        halves = []
        for h in (2 * pair, 2 * pair + 1):
            acc = acc_ref[h]
            o = acc[:, :LANES] / acc[:, LANES:]
            in_low = h < GROUP
            want_low = h % 2 == 0
            if in_low != want_low:
                o = pltpu.roll(o, HEAD_DIM, axis=1)
            halves.append(o)
        o_ref[:, pair * LANES:(pair + 1) * LANES] = jnp.where(is_low, halves[0], halves[1])


def _dsa_prompt_kernel(qi_ref, qa_ref, wt_ref, kiw_ref, ka_ref, va_ref, oa_ref,
                       sc_ref, wb_ref, mrun_ref, mb_ref, acc_ref):
    qi = pl.program_id(1)
    nc = (qi * TQ + TQ + CK - 1) // CK
    qpos = qi * TQ + lax.broadcasted_iota(jnp.int32, (TQ, CK), 0)
    lane = lax.broadcasted_iota(jnp.int32, (TQ, CK), 1)

    wt = wt_ref[...]
    for h in range(N_HEADS):
        wb_ref[h] = jnp.broadcast_to(wt[:, IDX_DIM + h:IDX_DIM + h + 1], (TQ, CK))

    def score_chunk(c, carry):
        mx, mn = carry
        kc = kiw_ref[pl.ds(pl.multiple_of(c * CK, CK), CK), :]
        acc = jnp.sum(wb_ref[...] * jnp.maximum(_scores(qi_ref, kc), 0.0), axis=0)
        valid = (c * CK + lane) <= qpos
        sc_ref[c] = jnp.where(valid, acc, -jnp.inf)
        mx = jnp.maximum(mx, jnp.where(valid, acc, -jnp.inf))
        mn = jnp.minimum(mn, jnp.where(valid, acc, jnp.inf))
        return mx, mn

    mx, mn = lax.fori_loop(0, nc, score_chunk,
                           (jnp.full((TQ, CK), -jnp.inf, F32), jnp.full((TQ, CK), jnp.inf, F32)))
    row_max = jnp.max(mx, axis=1, keepdims=True)
    row_min = jnp.min(mn, axis=1, keepdims=True)
    nvalid = (qpos[:, :1] + 1).astype(F32)
    kcol = jnp.minimum(nvalid, float(DSA_TOPK))
    t = _topk_threshold(sc_ref, nc, kcol, nvalid, row_max, row_min)

    tb = jnp.broadcast_to(t, (TQ, CK))

    def to_bias(c, carry):
        sc_ref[c] = jnp.where(sc_ref[c] > tb, 0.0, NEG)
        return carry

    lax.fori_loop(0, nc, to_bias, 0)

    @pl.when(nc % 2 == 1)
    def _():
        sc_ref[nc] = jnp.full((TQ, CK), NEG, F32)

    _two_pass_attention(nc, qa_ref, ka_ref, va_ref, lambda c: sc_ref[c], mrun_ref, mb_ref, acc_ref)
    _write_heads(oa_ref, acc_ref)


def _dsa_prompt(qi, qa, kiw32, kiw16, ka16, va16, n_batch, seq):
    nq = seq // TQ
    head_spec = pl.BlockSpec((N_HEADS, TQ, LANES), lambda n, i: (0, n * nq + i, 0))
    seq_spec = pl.BlockSpec((seq, LANES), lambda n, i: (n, 0))
    return pl.pallas_call(
        _dsa_prompt_kernel,
        grid=(n_batch, nq),
        in_specs=[head_spec, head_spec,
                  pl.BlockSpec((TQ, LANES), lambda n, i: (n * nq + i, 0)),
                  seq_spec, seq_spec,
                  pl.BlockSpec((seq, 2 * LANES), lambda n, i: (n, 0))],
        out_specs=pl.BlockSpec((TQ, BR_WIDTH), lambda n, i: (n * nq + i, 0)),
        out_shape=jax.ShapeDtypeStruct((n_batch * seq, BR_WIDTH), F32),
        scratch_shapes=[pltpu.VMEM((seq // CK, TQ, CK), F32),
                        pltpu.VMEM((N_HEADS, TQ, CK), F32),
                        pltpu.VMEM((N_HEADS, TQ, LANES), F32),
                        pltpu.VMEM((N_HEADS, TQ, CK), F32),
                        pltpu.VMEM((N_HEADS, TQ, 2 * LANES), F32)],
        compiler_params=pltpu.CompilerParams(vmem_limit_bytes=48 * 1024 * 1024),
        name="dsa_prompt",
    )(qi, qa, kiw32, kiw16, ka16, va16)


def _moba_select(gate, n_past):
    blk = lax.broadcasted_iota(jnp.int32, gate.shape, 1)
    blkf = blk.astype(F32)
    past = blk < n_past
    g = jnp.where(past, gate, -jnp.inf)
    sel = jnp.zeros(gate.shape, F32)
    for _ in range(MOBA_TOPK):
        top = jnp.max(g, axis=1, keepdims=True)
        first = jnp.min(jnp.where(g == top, blkf, float(LANES)), axis=1, keepdims=True)
        pick = blkf == first
        sel = jnp.where(pick, 1.0, sel)
        g = jnp.where(pick, -jnp.inf, g)
    return jnp.where(past, sel, 0.0)


def _moba_prompt_kernel(qb_ref, kb_ref, vb_ref, km_ref, ob_ref, q2_ref, mrun_ref, mb_ref, acc_ref):
    qi = pl.program_id(1)
    own = (qi * TQ) // MOBA_BLOCK
    km = km_ref[...].astype(BF16)
    blk = lax.broadcasted_iota(jnp.int32, (TQ, LANES), 1)
    for h in range(N_HEADS):
        q = qb_ref[h]
        sel = _moba_select(_dot_nt(q, km), own)
        allowed = jnp.logical_or(sel > 0.5, blk >= own)
        q2_ref[h] = jnp.concatenate([q, jnp.where(allowed, 0.0, NEG).astype(BF16)], axis=1)

    qpos = qi * TQ + lax.broadcasted_iota(jnp.int32, (TQ, CK), 0)
    lane = lax.broadcasted_iota(jnp.int32, (TQ, CK), 1)
    causal_bias = jnp.where((own * CK + lane) <= qpos, 0.0, NEG)

    def bias_of_block(b):
        return jnp.where(b < own, 0.0, jnp.where(b == own, causal_bias, NEG))

    _two_pass_attention(own + 1, q2_ref, kb_ref, vb_ref, bias_of_block, mrun_ref, mb_ref, acc_ref)
    _write_heads(ob_ref, acc_ref)


def _moba_prompt(qb, kb16, vb16, kmean, n_batch, seq):
    nq = seq // TQ
    head_spec = pl.BlockSpec((N_HEADS, TQ, LANES), lambda n, i: (0, n * nq + i, 0))
    seq_spec = pl.BlockSpec((seq, 2 * LANES), lambda n, i: (n, 0))
    return pl.pallas_call(
        _moba_prompt_kernel,
        grid=(n_batch, nq),
        in_specs=[head_spec, seq_spec, seq_spec,
                  pl.BlockSpec((None, LANES, LANES), lambda n, i: (n, 0, 0))],
        out_specs=pl.BlockSpec((TQ, BR_WIDTH), lambda n, i: (n * nq + i, 0)),
        out_shape=jax.ShapeDtypeStruct((n_batch * seq, BR_WIDTH), F32),
        scratch_shapes=[pltpu.VMEM((N_HEADS, TQ, 2 * LANES), BF16),
                        pltpu.VMEM((N_HEADS, TQ, LANES), F32),
                        pltpu.VMEM((N_HEADS, TQ, CK), F32),
                        pltpu.VMEM((N_HEADS, TQ, 2 * LANES), F32)],
        compiler_params=pltpu.CompilerParams(vmem_limit_bytes=40 * 1024 * 1024),
        name="moba_prompt",
    )(qb, kb16, vb16, kmean)


def _page_copy(cache_ref, buf_ref, sem_ref, page, slot, p):
    window = pl.ds(pl.multiple_of(p * PAGE, PAGE), PAGE)
    return pltpu.make_async_copy(cache_ref.at[page], buf_ref.at[slot, :, window], sem_ref.at[slot])


def _start_pages(pt_ref, seq_idx, slot, n_pages, streams):
    def body(p, carry):
        page = pt_ref[seq_idx, p]
        for cache_ref, buf_ref, sem_ref in streams:
            _page_copy(cache_ref, buf_ref, sem_ref, page, slot, p).start()
        return carry

    lax.fori_loop(0, n_pages, body, 0)


def _wait_pages(slot, n_pages, streams):
    def body(p, carry):
        for cache_ref, buf_ref, sem_ref in streams:
            _page_copy(cache_ref, buf_ref, sem_ref, 0, slot, p).wait()
        return carry

    lax.fori_loop(0, n_pages, body, 0)


def _gather_pipeline(pt_ref, n_pages, streams):
    s = pl.program_id(0)
    slot = s % 2

    @pl.when(s == 0)
    def _():
        _start_pages(pt_ref, s, slot, n_pages, streams)

    @pl.when(s + 1 < pl.num_programs(0))
    def _():
        _start_pages(pt_ref, s + 1, 1 - slot, n_pages, streams)

    _wait_pages(slot, n_pages, streams)
    return slot


def _dec_scores_kernel(pt_ref, q_ref, w_ref, knew_ref, cache_ref, o_ref, kbuf, sem):
    n_pages = pt_ref.shape[1]
    past = n_pages * PAGE
    slot = _gather_pipeline(pt_ref, n_pages, [(cache_ref, kbuf, sem)])
    q = q_ref[...]
    w = w_ref[...]
    qk = q[:, :IDX_DIM]
    for ch in range(past // DEC_CHUNK):
        kc = kbuf[slot, :, ch * DEC_CHUNK:(ch + 1) * DEC_CHUNK].astype(BF16)
        lg = _dot(qk, kc)
        o_ref[:, ch * DEC_CHUNK:(ch + 1) * DEC_CHUNK] = jnp.sum(
            w * jnp.maximum(lg, 0.0), axis=0, keepdims=True)
    knew = knew_ref[...].astype(BF16).astype(F32)
    lg_new = jnp.sum(q.astype(F32) * knew, axis=1, keepdims=True)
    s_new = jnp.sum(w * jnp.maximum(lg_new, 0.0), axis=0, keepdims=True)
    lane = lax.broadcasted_iota(jnp.int32, (1, CK), 1)
    o_ref[:, past:past + CK] = jnp.where(lane == 0, s_new, -jnp.inf)


def _dec_scores(page_table, qi_s, w_s, knew, cache):
    ns, n_pages = page_table.shape
    past = n_pages * PAGE
    width = past + CK
    grid_spec = pltpu.PrefetchScalarGridSpec(
        num_scalar_prefetch=1,
        grid=(ns,),
        in_specs=[pl.BlockSpec((None, N_HEADS, LANES), lambda s, pt: (s, 0, 0)),
                  pl.BlockSpec((None, N_HEADS, 1), lambda s, pt: (s, 0, 0)),
                  pl.BlockSpec((None, 1, LANES), lambda s, pt: (s, 0, 0)),
                  pl.BlockSpec(memory_space=pl.ANY)],
        out_specs=pl.BlockSpec((None, 1, width), lambda s, pt: (s, 0, 0)),
        scratch_shapes=[pltpu.VMEM((2, IDX_DIM, past), F32),
                        pltpu.SemaphoreType.DMA((2,))])
    return pl.pallas_call(
        _dec_scores_kernel,
        grid_spec=grid_spec,
        out_shape=jax.ShapeDtypeStruct((ns, 1, width), F32),
        compiler_params=pltpu.CompilerParams(dimension_semantics=("arbitrary",),
                                             vmem_limit_bytes=32 * 1024 * 1024),
        name="dec_scores",
    )(page_table, qi_s, w_s, knew, cache)


def _dec_select_kernel(s_ref, bias_ref, sc_ref):
    rows, width = s_ref.shape
    nch = width // CK
    mx = jnp.full((rows, CK), -jnp.inf, F32)
    mn = jnp.full((rows, CK), jnp.inf, F32)
    for c in range(nch):
        s = s_ref[:, c * CK:(c + 1) * CK]
        sc_ref[c] = s
        mx = jnp.maximum(mx, s)
        mn = jnp.minimum(mn, jnp.where(s > -jnp.inf, s, jnp.inf))
    row_max = jnp.max(mx, axis=1, keepdims=True)
    row_min = jnp.min(mn, axis=1, keepdims=True)
    nvalid = jnp.full((rows, 1), float(width - CK + 1), F32)
    kcol = jnp.minimum(nvalid, float(DSA_TOPK))
    t = _topk_threshold(sc_ref, nch, kcol, nvalid, row_max, row_min)
    tb = jnp.broadcast_to(t, (rows, CK))
    for c in range(nch):
        bias_ref[:, c * CK:(c + 1) * CK] = jnp.where(sc_ref[c] > tb, 0.0, NEG)


def _dec_select(scores):
    rows, width = scores.shape
    return pl.pallas_call(
        _dec_select_kernel,
        grid=(1,),
        in_specs=[pl.BlockSpec((rows, width), lambda i: (0, 0))],
        out_specs=pl.BlockSpec((rows, width), lambda i: (0, 0)),
        out_shape=jax.ShapeDtypeStruct((rows, width), F32),
        scratch_shapes=[pltpu.VMEM((width // CK, rows, CK), F32)],
        compiler_params=pltpu.CompilerParams(vmem_limit_bytes=40 * 1024 * 1024),
        name="dec_select",
    )(scores)


def _decode_scores(q, kbuf, slot, past, s_ref):
    for ch in range(past // DEC_CHUNK):
        win = slice(ch * DEC_CHUNK, (ch + 1) * DEC_CHUNK)
        s_ref[:, win] = _dot(q, kbuf[slot, :, win].astype(BF16))


def _decode_softmax_pv(q, vbuf, slot, past, s_ref, knew, vnew, bias_new, o_ref):
    n_ch = past // DEC_CHUNK
    s_new = jnp.sum(q.astype(F32) * knew.astype(BF16).astype(F32), axis=1, keepdims=True) + bias_new
    m = s_new
    for ch in range(n_ch):
        win = slice(ch * DEC_CHUNK, (ch + 1) * DEC_CHUNK)
        m = jnp.maximum(m, jnp.max(s_ref[:, win], axis=1, keepdims=True))
    p_new = jnp.exp2(s_new - m)
    l = p_new
    acc = p_new * vnew.astype(BF16).astype(F32)
    for ch in range(n_ch):
        win = slice(ch * DEC_CHUNK, (ch + 1) * DEC_CHUNK)
        p = jnp.exp2(s_ref[:, win] - m)
        l = l + jnp.sum(p, axis=1, keepdims=True)
        acc = acc + _dot_nt(p.astype(BF16), vbuf[slot, :, win].astype(BF16))
    o = acc / l
    head = lax.broadcasted_iota(jnp.int32, o.shape, 0)
    o_ref[...] = jnp.where(head < GROUP, o, pltpu.roll(o, HEAD_DIM, axis=1))


def _dec_dsa_kernel(pt_ref, q_ref, bias_ref, knew_ref, vnew_ref, kc_ref, vc_ref, o_ref,
                    kbuf, vbuf, s_ref, ksem, vsem):
    n_pages = pt_ref.shape[1]
    past = n_pages * PAGE
    slot = _gather_pipeline(pt_ref, n_pages, [(kc_ref, kbuf, ksem), (vc_ref, vbuf, vsem)])
    q = q_ref[...]
    _decode_scores(q, kbuf, slot, past, s_ref)
    for ch in range(past // DEC_CHUNK):
        win = slice(ch * DEC_CHUNK, (ch + 1) * DEC_CHUNK)
        s_ref[:, win] = s_ref[:, win] + bias_ref[:, win]
    _decode_softmax_pv(q, vbuf, slot, past, s_ref, knew_ref[...], vnew_ref[...],
                       bias_ref[:, past:past + 1], o_ref)


def _dec_moba_kernel(pt_ref, q_ref, knew_ref, vnew_ref, kc_ref, vc_ref, o_ref,
                     kbuf, vbuf, s_ref, ksem, vsem):
    n_pages = pt_ref.shape[1]
    past = n_pages * PAGE
    n_blocks = past // MOBA_BLOCK
    slot = _gather_pipeline(pt_ref, n_pages, [(kc_ref, kbuf, ksem), (vc_ref, vbuf, vsem)])
    q = q_ref[...]
    _decode_scores(q, kbuf, slot, past, s_ref)
    lane = lax.broadcasted_iota(jnp.int32, (N_HEADS, LANES), 1)
    gate = jnp.zeros((N_HEADS, LANES), F32)
    for b in range(n_blocks):
        blk_sum = jnp.sum(s_ref[:, b * MOBA_BLOCK:(b + 1) * MOBA_BLOCK], axis=1, keepdims=True)
        gate = jnp.where(lane == b, blk_sum * (1.0 / MOBA_BLOCK), gate)
    sel = _moba_select(gate, n_blocks)
    for b in range(n_blocks):
        win = slice(b * MOBA_BLOCK, (b + 1) * MOBA_BLOCK)
        s_ref[:, win] = s_ref[:, win] + (sel[:, b:b + 1] - 1.0) * (-NEG)
    _decode_softmax_pv(q, vbuf, slot, past, s_ref, knew_ref[...], vnew_ref[...], 0.0, o_ref)


def _dec_attend_call(kernel, name, page_table, q_s, extra, knew, vnew, k_cache, v_cache):
    ns, n_pages = page_table.shape
    past = n_pages * PAGE
    per_seq = lambda shape: pl.BlockSpec((None,) + shape, lambda s, pt: (s, 0, 0))
    in_specs = [per_seq((N_HEADS, LANES))]
    in_specs += [per_seq((1, e.shape[2])) for e in extra]
    in_specs += [per_seq((1, LANES)), per_seq((1, LANES)),
                 pl.BlockSpec(memory_space=pl.ANY), pl.BlockSpec(memory_space=pl.ANY)]
    grid_spec = pltpu.PrefetchScalarGridSpec(
        num_scalar_prefetch=1,
        grid=(ns,),
        in_specs=in_specs,
        out_specs=per_seq((N_HEADS, LANES)),
        scratch_shapes=[pltpu.VMEM((2, LANES, past), F32),
                        pltpu.VMEM((2, LANES, past), F32),
                        pltpu.VMEM((N_HEADS, past), F32),
                        pltpu.SemaphoreType.DMA((2,)), pltpu.SemaphoreType.DMA((2,))])
    return pl.pallas_call(
        kernel,
        grid_spec=grid_spec,
        out_shape=jax.ShapeDtypeStruct((ns, N_HEADS, LANES), F32),
        compiler_params=pltpu.CompilerParams(dimension_semantics=("arbitrary",),
                                             vmem_limit_bytes=48 * 1024 * 1024),
        name=name,
    )(page_table, q_s, *extra, knew, vnew, k_cache, v_cache)


def _heads_from_padded(o):
    return o[:, :, :HEAD_DIM].reshape(o.shape[0], BR_WIDTH)


def _finish_kernel(x_ref, oa_ref, ob_ref, za_ref, zb_ref, ga_ref, gb_ref, ada_ref,
                   wa_ref, wb_ref, wo_ref, gf_ref, y_ref):
    ya = _dot((oa_ref[...] * za_ref[...].astype(F32)).astype(BF16), wa_ref[...])
    yb = _dot((ob_ref[...] * zb_ref[...].astype(F32)).astype(BF16), wb_ref[...])
    m = ga_ref[...].astype(F32) * ya + gb_ref[...].astype(F32) * yb
    gate = ada_ref[...][:, 2 * D_MODEL:]
    xo = x_ref[...] + gate * _dot(m.astype(BF16), wo_ref[...])
    ms = jnp.mean(xo * xo, axis=-1, keepdims=True)
    y_ref[...] = (xo * lax.rsqrt(ms + EPS)) * gf_ref[...]


def _finish(x, oa, ob, za, zb, ga, gb, ada, wa, wb, wo, gf, tm, rows_per_ada):
    rows = x.shape[0]
    tiles_per_ada = rows_per_ada // tm
    a_rows = ada.shape[1]
    row_spec = lambda w: pl.BlockSpec((tm, w), lambda i: (i, 0))
    full = lambda a: pl.BlockSpec(a.shape, lambda i: (0, 0))
    return pl.pallas_call(
        _finish_kernel,
        grid=(rows // tm,),
        in_specs=[row_spec(D_MODEL), row_spec(BR_WIDTH), row_spec(BR_WIDTH),
                  row_spec(BR_WIDTH), row_spec(BR_WIDTH), row_spec(D_MODEL), row_spec(D_MODEL),
                  pl.BlockSpec((None, a_rows, 3 * D_MODEL), lambda i: (i // tiles_per_ada, 0, 0)),
                  full(wa), full(wb), full(wo), full(gf)],
        out_specs=row_spec(D_MODEL),
        out_shape=jax.ShapeDtypeStruct((rows, D_MODEL), F32),
        compiler_params=pltpu.CompilerParams(vmem_limit_bytes=40 * 1024 * 1024),
        name="finish",
    )(x, oa, ob, za, zb, ga, gb, ada, wa, wb, wo, gf)


def _rope_tables(pos):
    half = HEAD_DIM // 2
    freqs = ROPE_THETA ** (-jnp.arange(half, dtype=F32) / half)
    ang = pos.astype(F32)[:, None] * freqs[None, :]
    cos = jnp.cos(ang)
    sin = jnp.sin(ang)
    cos_t = jnp.concatenate([cos, cos, cos, cos], axis=1)
    sin_t = jnp.concatenate([-sin, sin, -sin, sin], axis=1)
    return cos_t, sin_t


def kernel(x_prompt, x_sample, cache_k_a, cache_v_a, cache_k_idx, cache_k_b, cache_v_b, page_table,
           c_prompt, c_sample, w_ada, b_ada, w_in, w_br_a, w_br_b, w_o, g_final):
    n_batch, seq, _ = x_prompt.shape
    ns, dec_seq, _ = x_sample.shape
    depth = w_in.shape[0]
    assert depth == 1 and dec_seq == 1 and seq % MOBA_BLOCK == 0
    n_pages = page_table.shape[1]
    past = n_pages * PAGE
    n_pool = cache_k_a.shape[1]
    tm = MOBA_BLOCK

    pad_rows = (-(n_batch + ns)) % 8
    c_all = jnp.concatenate([c_prompt, c_sample, jnp.zeros((pad_rows, D_MODEL), F32)], axis=0)
    ada = _ada(c_all, w_ada[0], b_ada[0][None, :])
    ada_p = ada[:n_batch].reshape(n_batch, 1, 3 * D_MODEL)
    ada_s = ada[n_batch:n_batch + ns].reshape(1, ns, 3 * D_MODEL)

    w_packed = jnp.concatenate(
        [w_in[0][:, :_ORIG_SPLIT], jnp.zeros((D_MODEL, _PAD_COLS), F32), w_in[0][:, _ORIG_SPLIT:]],
        axis=1).astype(BF16)
    cos_p, sin_p = _rope_tables(jnp.arange(seq, dtype=jnp.int32))
    cos_s, sin_s = _rope_tables(jnp.full((ns,), past, dtype=jnp.int32))

    (qa, qi, qb, ka32, va32, kiw32, kb32, vb32, ka16, va16, kiw16, kb16, vb16,
     za, zb, ga, gb, kmean) = _prepare(x_prompt.reshape(n_batch * seq, D_MODEL), ada_p,
                                       cos_p, sin_p, w_packed, tm, seq)
    oa = _dsa_prompt(qi, qa, kiw32, kiw16, ka16, va16, n_batch, seq)
    n_blk = seq // MOBA_BLOCK
    km_pad = jnp.pad(kmean.reshape(n_batch, n_blk, LANES), ((0, 0), (0, LANES - n_blk), (0, 0)))
    ob = _moba_prompt(qb, kb16, vb16, km_pad, n_batch, seq)
    wa = w_br_a[0].astype(BF16)
    wb = w_br_b[0].astype(BF16)
    wo = w_o[0].astype(BF16)
    gf = g_final[None, :]
    y_p = _finish(x_prompt.reshape(n_batch * seq, D_MODEL), oa, ob, za, zb, ga, gb, ada_p,
                  wa, wb, wo, gf, tm, seq)

    (qa_s, qi_s, qb_s, ka32_s, va32_s, kiw32_s, kb32_s, vb32_s, _, _, _, _, _,
     za_s, zb_s, ga_s, gb_s, _) = _prepare(x_sample.reshape(ns, D_MODEL), ada_s,
                                           cos_s, sin_s, w_packed, ns, ns)
    per_seq_heads = lambda q: jnp.transpose(q, (1, 0, 2))
    row3 = lambda a: a.reshape(ns, 1, LANES)
    w_s = kiw32_s[:, IDX_DIM:IDX_DIM + N_HEADS].reshape(ns, N_HEADS, 1)
    kv_pages = lambda c: jnp.transpose(c[0], (0, 2, 3, 1)).reshape(n_pool, LANES, PAGE)
    idx_pages = jnp.transpose(cache_k_idx[0], (0, 2, 1))
    scores = _dec_scores(page_table, per_seq_heads(qi_s), w_s, row3(kiw32_s), idx_pages)
    bias = _dec_select(scores.reshape(ns, past + CK)).reshape(ns, 1, past + CK)
    oa_s = _dec_attend_call(_dec_dsa_kernel, "dec_dsa", page_table, per_seq_heads(qa_s), [bias],
                            row3(ka32_s), row3(va32_s), kv_pages(cache_k_a), kv_pages(cache_v_a))
    ob_s = _dec_attend_call(_dec_moba_kernel, "dec_moba", page_table, per_seq_heads(qb_s), [],
                            row3(kb32_s), row3(vb32_s), kv_pages(cache_k_b), kv_pages(cache_v_b))
    y_s = _finish(x_sample.reshape(ns, D_MODEL), _heads_from_padded(oa_s), _heads_from_padded(ob_s),
                  za_s, zb_s, ga_s, gb_s, ada_s, wa, wb, wo, gf, ns, ns)

    kv = lambda a, n, t: a.reshape(1, n, t, KV_HEADS, HEAD_DIM)
    ki = lambda a, n, t: a[:, :IDX_DIM].reshape(1, n, t, IDX_DIM)
    return (y_p.reshape(n_batch, seq, D_MODEL), y_s.reshape(ns, 1, D_MODEL),
            kv(ka32, n_batch, seq), kv(va32, n_batch, seq), ki(kiw32, n_batch, seq),
            kv(kb32, n_batch, seq), kv(vb32, n_batch, seq),
            kv(ka32_s, ns, 1), kv(va32_s, ns, 1), ki(kiw32_s, ns, 1),
            kv(kb32_s, ns, 1), kv(vb32_s, ns, 1))
```

```python
import functools

import jax
import jax.numpy as jnp
from jax import lax
from jax.experimental import pallas as pl
from jax.experimental.pallas import tpu as pltpu

F32 = jnp.float32
BF16 = jnp.bfloat16

D_MODEL = 1024
HEAD_DIM = 64
N_HEADS = 8
KV_HEADS = 2
GROUP = N_HEADS // KV_HEADS
IDX_DIM = 64
BR_WIDTH = N_HEADS * HEAD_DIM
DSA_TOPK = 256
MOBA_BLOCK = 256
MOBA_TOPK = 3
PAGE = 128
ROPE_THETA = 10000.0
EPS = 1e-6
LANES = 128
CK = 256
TQ = 256
NEG = -1e30
DEC_CHUNK = 1024

_ORIG_SPLIT = 1864
_PAD_COLS = 56
C_QA, C_KA, C_VA, C_ZA, C_QI, C_KIW, C_QB, C_KB, C_VB, C_ZB, C_GA, C_GB, C_END = (
    0, 512, 640, 768, 1280, 1792, 1920, 2432, 2560, 2688, 3200, 4224, 5248)

LOG2E = 1.4426950408889634
_ATT_WIDTHS = (LANES, 2 * LANES, LANES, 2 * LANES, 2 * LANES)

_NT = (((1,), (1,)), ((), ()))


def _dot_nt(a, b):
    return lax.dot_general(a, b, _NT, preferred_element_type=F32)


def _dot(a, b):
    return jnp.dot(a, b, preferred_element_type=F32)


def _ada_kernel(c_ref, w_ref, b_ref, o_ref):
    o_ref[...] = _dot(c_ref[...].astype(BF16), w_ref[...].astype(BF16)) + b_ref[...]


def _ada(c_all, w_ada, b_ada):
    rows = c_all.shape[0]
    return pl.pallas_call(
        _ada_kernel,
        grid=(3,),
        in_specs=[pl.BlockSpec((rows, D_MODEL), lambda j: (0, 0)),
                  pl.BlockSpec((D_MODEL, D_MODEL), lambda j: (0, j)),
                  pl.BlockSpec((1, D_MODEL), lambda j: (0, j))],
        out_specs=pl.BlockSpec((rows, D_MODEL), lambda j: (0, j)),
        out_shape=jax.ShapeDtypeStruct((rows, 3 * D_MODEL), F32),
        name="ada",
    )(c_all, w_ada, b_ada)


def _rope(x, cos, sin_signed):
    w = x.shape[1]
    reps = w // LANES
    lane = lax.broadcasted_iota(jnp.int32, x.shape, 1)
    upper = (lane % HEAD_DIM) >= (HEAD_DIM // 2)
    from_below = pltpu.roll(x, HEAD_DIM // 2, axis=1)
    from_above = pltpu.roll(x, w - HEAD_DIM // 2, axis=1)
    partner = jnp.where(upper, from_below, from_above)
    if reps > 1:
        cos = jnp.concatenate([cos] * reps, axis=1)
        sin_signed = jnp.concatenate([sin_signed] * reps, axis=1)
    return x * cos + partner * sin_signed


def _store_padded_heads(q, out_ref, low_for_head):
    lane = lax.broadcasted_iota(jnp.int32, (q.shape[0], LANES), 1)
    is_low = lane < HEAD_DIM
    for h in range(N_HEADS):
        col = q[:, (h // 2) * LANES:(h // 2 + 1) * LANES]
        want_low = low_for_head(h)
        if (h % 2 == 0) != want_low:
            col = pltpu.roll(col, HEAD_DIM, axis=1)
        keep = is_low if want_low else jnp.logical_not(is_low)
        out_ref[h] = jnp.where(keep, col, 0.0).astype(BF16)


def _prep_kernel(x_ref, ada_ref, cos_ref, sin_ref, w_ref,
                 qa_ref, qi_ref, qb_ref,
                 ka32_ref, va32_ref, kiw32_ref, kb32_ref, vb32_ref,
                 ka16_ref, va16_ref, kiw16_ref, kb16_ref, vb16_ref,
                 za_ref, zb_ref, ga_ref, gb_ref, km_ref, *, tiles_per_seq):
    x = x_ref[...]
    ada = ada_ref[...]
    shift = ada[:, :D_MODEL]
    scale = ada[:, D_MODEL:2 * D_MODEL]
    ms = jnp.mean(x * x, axis=-1, keepdims=True)
    h = (x * lax.rsqrt(ms + EPS)) * (1.0 + scale) + shift
    hb = h.astype(BF16)
    cos = cos_ref[...]
    sin = sin_ref[...]

    def proj(a, b):
        return _dot(hb, w_ref[:, a:b])

    kv_low = lambda hh: hh < GROUP
    qscale = HEAD_DIM ** -0.5 * LOG2E
    _store_padded_heads(_rope(proj(C_QA, C_KA), cos, sin) * qscale, qa_ref, kv_low)
    _store_padded_heads(_rope(proj(C_QI, C_KIW), cos, sin) * (IDX_DIM ** -0.5), qi_ref,
                        lambda hh: True)
    _store_padded_heads(_rope(proj(C_QB, C_KB), cos, sin) * qscale, qb_ref, kv_low)

    ka = _rope(proj(C_KA, C_VA), cos, sin)
    ka32_ref[...] = ka
    ka16_ref[...] = ka.astype(BF16)
    va = proj(C_VA, C_ZA)
    va32_ref[...] = va
    ones = jnp.ones(va.shape, BF16)
    va16_ref[...] = jnp.concatenate([va.astype(BF16), ones], axis=1)

    kiw = proj(C_KIW, C_QB)
    lane = lax.broadcasted_iota(jnp.int32, kiw.shape, 1)
    kiw = jnp.where(lane < IDX_DIM, _rope(kiw, cos, sin), kiw * (N_HEADS ** -0.5))
    kiw32_ref[...] = kiw
    kiw16_ref[...] = kiw.astype(BF16)

    kb = _rope(proj(C_KB, C_VB), cos, sin)
    kb32_ref[...] = kb
    own_block = pl.program_id(0) % tiles_per_seq
    block_code = jnp.where(lane == own_block, 1.0, 0.0).astype(BF16)
    kb16_ref[...] = jnp.concatenate([kb.astype(BF16), block_code], axis=1)
    km_ref[0] = jnp.sum(kb, axis=0, keepdims=True) * (1.0 / MOBA_BLOCK)
    vb = proj(C_VB, C_ZB)
    vb32_ref[...] = vb
    vb16_ref[...] = jnp.concatenate([vb.astype(BF16), ones], axis=1)

    za_ref[...] = jax.nn.silu(proj(C_ZA, C_QI)).astype(BF16)
    zb_ref[...] = jax.nn.silu(proj(C_ZB, C_GA)).astype(BF16)
    ga_ref[...] = jax.nn.sigmoid(proj(C_GA, C_GB)).astype(BF16)
    gb_ref[...] = jax.nn.sigmoid(proj(C_GB, C_END)).astype(BF16)


def _prepare(x, ada, cos, sin, w_packed, tm, rows_per_ada):
    rows = x.shape[0]
    nt = rows // tm
    tab_tiles = cos.shape[0] // tm
    tiles_per_ada = rows_per_ada // tm
    a_rows = ada.shape[1]
    row_spec = lambda w: pl.BlockSpec((tm, w), lambda i: (i, 0))
    head_spec = pl.BlockSpec((N_HEADS, tm, LANES), lambda i: (0, i, 0))
    sd = jax.ShapeDtypeStruct
    out_shape = (
        [sd((N_HEADS, rows, LANES), BF16)] * 3
        + [sd((rows, LANES), F32)] * 5
        + [sd((rows, w), BF16) for w in _ATT_WIDTHS]
        + [sd((rows, BR_WIDTH), BF16)] * 2
        + [sd((rows, D_MODEL), BF16)] * 2
        + [sd((nt, 1, LANES), F32)])
    out_specs = (
        [head_spec] * 3 + [row_spec(LANES)] * 5 + [row_spec(w) for w in _ATT_WIDTHS]
        + [row_spec(BR_WIDTH)] * 2
        + [row_spec(D_MODEL)] * 2 + [pl.BlockSpec((1, 1, LANES), lambda i: (i, 0, 0))])
    return pl.pallas_call(
        functools.partial(_prep_kernel, tiles_per_seq=tiles_per_ada),
        grid=(nt,),
        in_specs=[row_spec(D_MODEL),
                  pl.BlockSpec((None, a_rows, 3 * D_MODEL), lambda i: (i // tiles_per_ada, 0, 0)),
                  pl.BlockSpec((tm, LANES), lambda i: (i % tab_tiles, 0)),
                  pl.BlockSpec((tm, LANES), lambda i: (i % tab_tiles, 0)),
                  pl.BlockSpec((D_MODEL, C_END), lambda i: (0, 0))],
        out_specs=out_specs,
        out_shape=out_shape,
        compiler_params=pltpu.CompilerParams(vmem_limit_bytes=56 * 1024 * 1024),
        name="prepare",
    )(x, ada, cos, sin, w_packed)


def _topk_threshold(sc_ref, nc, kcol, nvalid):
    rows = sc_ref.shape[1]
    n_pairs = (nc + 1) // 2

    band = 128
    bands = [slice(r, r + band) for r in range(0, rows, band)]
    halves = (slice(0, LANES), slice(LANES, CK))

    def count_gt(tcol):
        parts = []
        for rs in bands:
            tb = jnp.broadcast_to(tcol[rs], (band, LANES))

            def body(i, acc, rs=rs, tb=tb):
                for c in (2 * i, 2 * i + 1):
                    for ls in halves:
                        acc = acc + jnp.where(sc_ref[c, rs, ls] > tb, 1.0, 0.0)
                return acc

            acc = lax.fori_loop(0, n_pairs, body, jnp.zeros((band, LANES), F32))
            parts.append(jnp.sum(acc, axis=1, keepdims=True))
        return jnp.concatenate(parts, axis=0)

    pos_parts, nonneg_parts, neg_parts = [], [], []
    for rs in bands:
        def zero_stats(c, carry, rs=rs):
            n_pos, n_nonneg, neg_max = carry
            for ls in halves:
                s = sc_ref[c, rs, ls]
                n_pos = n_pos + jnp.where(s > 0.0, 1.0, 0.0)
                n_nonneg = n_nonneg + jnp.where(s >= 0.0, 1.0, 0.0)
                neg_max = jnp.maximum(neg_max, jnp.where(s < 0.0, s, -jnp.inf))
            return n_pos, n_nonneg, neg_max

        zl = jnp.zeros((band, LANES), F32)
        n_pos, n_nonneg, neg_max = lax.fori_loop(
            0, nc, zero_stats, (zl, zl, jnp.full((band, LANES), -jnp.inf, F32)))
        pos_parts.append(jnp.sum(n_pos, axis=1, keepdims=True))
        nonneg_parts.append(jnp.sum(n_nonneg, axis=1, keepdims=True))
        neg_parts.append(jnp.max(neg_max, axis=1, keepdims=True))
    c_pos = jnp.concatenate(pos_parts, axis=0)
    c_nonneg = jnp.concatenate(nonneg_parts, axis=0)
    neg_max = jnp.concatenate(neg_parts, axis=0)

    max_parts, min_parts = [], []
    for rs in bands:
        def extremes(c, carry, rs=rs):
            vmax, vmin = carry
            for ls in halves:
                s = sc_ref[c, rs, ls]
                vmax = jnp.maximum(vmax, s)
                vmin = jnp.minimum(vmin, jnp.where(s > -jnp.inf, s, jnp.inf))
            return vmax, vmin

        vmax, vmin = lax.fori_loop(0, nc, extremes, (jnp.full((band, LANES), -jnp.inf, F32),
                                                     jnp.full((band, LANES), jnp.inf, F32)))
        max_parts.append(jnp.max(vmax, axis=1, keepdims=True))
        min_parts.append(jnp.min(vmin, axis=1, keepdims=True))
    row_max = jnp.concatenate(max_parts, axis=0)
    row_min = jnp.concatenate(min_parts, axis=0)

    lo0 = row_min - 1.0 - jnp.abs(row_min) * (2.0 ** -10)
    below_zero = jnp.where(neg_max > -jnp.inf, neg_max, lo0)
    zeros = jnp.zeros_like(row_max)
    take_all = nvalid <= kcol
    open_row = jnp.logical_not(take_all)
    hit_zero = jnp.logical_and(open_row, c_pos == kcol)
    tie_zero = jnp.logical_and(open_row, jnp.logical_and(c_pos < kcol, c_nonneg >= kcol))
    above = jnp.logical_and(open_row, c_pos > kcol)
    below = jnp.logical_and(open_row, c_nonneg < kcol)
    lo_init = jnp.where(above, 0.0, jnp.where(tie_zero, below_zero, lo0))
    hi_init = jnp.where(above, row_max, jnp.where(below, below_zero, 0.0))
    c_hi_init = jnp.where(below, c_nonneg, jnp.where(tie_zero, c_pos, 0.0))
    t_init = jnp.where(hit_zero, 0.0, lo0)
    active0 = jnp.where(jnp.logical_or(above, below), 1.0, 0.0)
    tie0 = jnp.where(tie_zero, 1.0, 0.0)
    init = (lo_init, hi_init, t_init, c_hi_init, active0, tie0,
            jnp.sum(active0).astype(jnp.int32), jnp.int32(0))

    def cond(st):
        return jnp.logical_and(st[6] > 0, st[7] < 400)

    def body(st):
        lo, hi, t, c_hi, active, tie, _, it = st
        mid = lo + 0.5 * (hi - lo)
        c = count_gt(mid)
        on = active > 0.5
        hit = c == kcol
        spent = jnp.logical_or(mid <= lo, mid >= hi)
        go_up = jnp.logical_and(on, c > kcol)
        go_down = jnp.logical_and(on, c < kcol)
        finish_hit = jnp.logical_and(on, hit)
        finish_tie = jnp.logical_and(on, jnp.logical_and(spent, jnp.logical_not(hit)))
        t = jnp.where(finish_hit, mid, t)
        tie = jnp.where(finish_tie, 1.0, tie)
        move_lo = jnp.logical_and(go_up, jnp.logical_not(finish_tie))
        move_hi = jnp.logical_and(go_down, jnp.logical_not(finish_tie))
        lo = jnp.where(move_lo, mid, lo)
        c_hi = jnp.where(move_hi, c, c_hi)
        hi = jnp.where(move_hi, mid, hi)
        active = jnp.where(jnp.logical_or(finish_hit, finish_tie), 0.0, active)
        return (lo, hi, t, c_hi, active, tie, jnp.sum(active).astype(jnp.int32), it + 1)

    lo, hi, t, c_hi, _, tie, _, _ = lax.while_loop(cond, body, init)
    is_tie = tie > 0.5
    t = jnp.where(is_tie, lo, t)

    @pl.when(jnp.sum(tie) > 0.5)
    def _():
        need = kcol - c_hi
        ri = lax.broadcasted_iota(jnp.int32, (CK, CK), 0)
        ci = lax.broadcasted_iota(jnp.int32, (CK, CK), 1)
        tri = jnp.where(ri <= ci, 1.0, 0.0).astype(BF16)
        hib = jnp.broadcast_to(hi, (rows, CK))
        tieb = jnp.broadcast_to(tie, (rows, CK)) > 0.5

        def fix(c, before):
            s = sc_ref[c]
            eq = jnp.where(jnp.logical_and(s == hib, tieb), 1.0, 0.0)
            rank = before + _dot(eq.astype(BF16), tri)
            drop = jnp.logical_and(eq > 0.5, rank > need)
            sc_ref[c] = jnp.where(drop, -jnp.inf, s)
            return before + jnp.sum(eq, axis=1, keepdims=True)

        lax.fori_loop(0, nc, fix, jnp.zeros_like(hi))

    return t


def _scores(q_ref, kc):
    q = q_ref[...].reshape(N_HEADS * TQ, q_ref.shape[2])
    return _dot_nt(q, kc).reshape(N_HEADS, TQ, CK)


def _track_max(s, mrun_ref):
    mrun_ref[...] = jnp.maximum(mrun_ref[...], jnp.maximum(s[:, :, :LANES], s[:, :, LANES:]))


def _finish_max(mrun_ref, mb_ref):
    m = jnp.max(mrun_ref[...], axis=2, keepdims=True)
    mb_ref[...] = jnp.broadcast_to(m, mb_ref.shape)


def _accumulate_pv(s, vc, mb_ref, acc_ref):
    p = jnp.exp2(s - mb_ref[...]).astype(BF16).reshape(N_HEADS * TQ, CK)
    acc_ref[...] = acc_ref[...] + _dot(p, vc).reshape(acc_ref.shape)


def _two_pass_attention(n_chunks, q_ref, k_ref, v_ref, bias_of_chunk, mrun_ref, mb_ref, acc_ref):
    mrun_ref[...] = jnp.full(mrun_ref.shape, NEG, F32)
    acc_ref[...] = jnp.zeros(acc_ref.shape, F32)
    n_pairs = (n_chunks + 1) // 2

    def max_pair(i, carry):
        for j in range(2):
            c = 2 * i + j
            kc = k_ref[pl.ds(pl.multiple_of(c * CK, CK), CK), :]
            _track_max(_scores(q_ref, kc) + bias_of_chunk(c)[None], mrun_ref)
        return carry

    lax.fori_loop(0, n_pairs, max_pair, 0)
    _finish_max(mrun_ref, mb_ref)

    def attend_pair(i, carry):
        for j in range(2):
            c = 2 * i + j
            off = pl.multiple_of(c * CK, CK)
            _accumulate_pv(_scores(q_ref, k_ref[pl.ds(off, CK), :]) + bias_of_chunk(c)[None],
                           v_ref[pl.ds(off, CK), :], mb_ref, acc_ref)
        return carry

    lax.fori_loop(0, n_pairs, attend_pair, 0)


def _write_heads(o_ref, acc_ref):
    lane = lax.broadcasted_iota(jnp.int32, (o_ref.shape[0], LANES), 1)
    is_low = lane < HEAD_DIM
    for pair in range(N_HEADS // 2):
        halves = []
        for h in (2 * pair, 2 * pair + 1):
            acc = acc_ref[h]
            o = acc[:, :LANES] / acc[:, LANES:]
            in_low = h < GROUP
            want_low = h % 2 == 0
            if in_low != want_low:
                o = pltpu.roll(o, HEAD_DIM, axis=1)
            halves.append(o)
        o_ref[:, pair * LANES:(pair + 1) * LANES] = jnp.where(is_low, halves[0], halves[1])


def _dsa_prompt_kernel(qi_ref, qa_ref, wt_ref, kiw_ref, ka_ref, va_ref, oa_ref,
                       sc_ref, wb_ref, mrun_ref, mb_ref, acc_ref):
    qi = pl.program_id(1)
    nc = (qi * TQ + TQ + CK - 1) // CK
    qpos = qi * TQ + lax.broadcasted_iota(jnp.int32, (TQ, CK), 0)
    lane = lax.broadcasted_iota(jnp.int32, (TQ, CK), 1)

    wt = wt_ref[...]
    for h in range(N_HEADS):
        wb_ref[h] = jnp.broadcast_to(wt[:, IDX_DIM + h:IDX_DIM + h + 1], (TQ, CK))

    def score_chunk(c, carry):
        kc = kiw_ref[pl.ds(pl.multiple_of(c * CK, CK), CK), :]
        acc = jnp.sum(wb_ref[...] * jnp.maximum(_scores(qi_ref, kc), 0.0), axis=0)
        sc_ref[c] = jnp.where((c * CK + lane) <= qpos, acc, -jnp.inf)
        return carry

    lax.fori_loop(0, nc, score_chunk, 0)
    nvalid = (qpos[:, :1] + 1).astype(F32)
    kcol = jnp.minimum(nvalid, float(DSA_TOPK))

    @pl.when(nc % 2 == 1)
    def _():
        sc_ref[nc] = jnp.full((TQ, CK), -jnp.inf, F32)

    t = _topk_threshold(sc_ref, nc, kcol, nvalid)

    tb = jnp.broadcast_to(t, (TQ, CK))

    def to_bias(c, carry):
        sc_ref[c] = jnp.where(sc_ref[c] > tb, 0.0, NEG)
        return carry

    lax.fori_loop(0, nc, to_bias, 0)

    @pl.when(nc % 2 == 1)
    def _():
        sc_ref[nc] = jnp.full((TQ, CK), NEG, F32)

    _two_pass_attention(nc, qa_ref, ka_ref, va_ref, lambda c: sc_ref[c], mrun_ref, mb_ref, acc_ref)
    _write_heads(oa_ref, acc_ref)


def _dsa_prompt(qi, qa, kiw32, kiw16, ka16, va16, n_batch, seq):
    nq = seq // TQ
    head_spec = pl.BlockSpec((N_HEADS, TQ, LANES), lambda n, i: (0, n * nq + i, 0))
    seq_spec = pl.BlockSpec((seq, LANES), lambda n, i: (n, 0))
    return pl.pallas_call(
        _dsa_prompt_kernel,
        grid=(n_batch, nq),
        in_specs=[head_spec, head_spec,
                  pl.BlockSpec((TQ, LANES), lambda n, i: (n * nq + i, 0)),
                  seq_spec, seq_spec,
                  pl.BlockSpec((seq, 2 * LANES), lambda n, i: (n, 0))],
        out_specs=pl.BlockSpec((TQ, BR_WIDTH), lambda n, i: (n * nq + i, 0)),
        out_shape=jax.ShapeDtypeStruct((n_batch * seq, BR_WIDTH), F32),
        scratch_shapes=[pltpu.VMEM((seq // CK, TQ, CK), F32),
                        pltpu.VMEM((N_HEADS, TQ, CK), F32),
                        pltpu.VMEM((N_HEADS, TQ, LANES), F32),
                        pltpu.VMEM((N_HEADS, TQ, CK), F32),
                        pltpu.VMEM((N_HEADS, TQ, 2 * LANES), F32)],
        compiler_params=pltpu.CompilerParams(vmem_limit_bytes=56 * 1024 * 1024),
        name="dsa_prompt",
    )(qi, qa, kiw32, kiw16, ka16, va16)


def _moba_select(gate, n_past):
    blk = lax.broadcasted_iota(jnp.int32, gate.shape, 1)
    blkf = blk.astype(F32)
    past = blk < n_past
    g = jnp.where(past, gate, -jnp.inf)
    sel = jnp.zeros(gate.shape, F32)
    for _ in range(MOBA_TOPK):
        top = jnp.max(g, axis=1, keepdims=True)
        first = jnp.min(jnp.where(g == top, blkf, float(LANES)), axis=1, keepdims=True)
        pick = blkf == first
        sel = jnp.where(pick, 1.0, sel)
        g = jnp.where(pick, -jnp.inf, g)
    return jnp.where(past, sel, 0.0)


def _moba_prompt_kernel(qb_ref, kb_ref, vb_ref, km_ref, ob_ref, q2_ref, mrun_ref, mb_ref, acc_ref):
    qi = pl.program_id(1)
    own = (qi * TQ) // MOBA_BLOCK
    km = km_ref[...].astype(BF16)
    blk = lax.broadcasted_iota(jnp.int32, (TQ, LANES), 1)
    for h in range(N_HEADS):
        q = qb_ref[h]
        sel = _moba_select(_dot_nt(q, km), own)
        allowed = jnp.logical_or(sel > 0.5, blk >= own)
        q2_ref[h] = jnp.concatenate([q, jnp.where(allowed, 0.0, NEG).astype(BF16)], axis=1)

    qpos = qi * TQ + lax.broadcasted_iota(jnp.int32, (TQ, CK), 0)
    lane = lax.broadcasted_iota(jnp.int32, (TQ, CK), 1)
    causal_bias = jnp.where((own * CK + lane) <= qpos, 0.0, NEG)

    def bias_of_block(b):
        return jnp.where(b < own, 0.0, jnp.where(b == own, causal_bias, NEG))

    _two_pass_attention(own + 1, q2_ref, kb_ref, vb_ref, bias_of_block, mrun_ref, mb_ref, acc_ref)
    _write_heads(ob_ref, acc_ref)


def _moba_prompt(qb, kb16, vb16, kmean, n_batch, seq):
    nq = seq // TQ
    head_spec = pl.BlockSpec((N_HEADS, TQ, LANES), lambda n, i: (0, n * nq + i, 0))
    seq_spec = pl.BlockSpec((seq, 2 * LANES), lambda n, i: (n, 0))
    return pl.pallas_call(
        _moba_prompt_kernel,
        grid=(n_batch, nq),
        in_specs=[head_spec, seq_spec, seq_spec,
                  pl.BlockSpec((None, LANES, LANES), lambda n, i: (n, 0, 0))],
        out_specs=pl.BlockSpec((TQ, BR_WIDTH), lambda n, i: (n * nq + i, 0)),
        out_shape=jax.ShapeDtypeStruct((n_batch * seq, BR_WIDTH), F32),
        scratch_shapes=[pltpu.VMEM((N_HEADS, TQ, 2 * LANES), BF16),
                        pltpu.VMEM((N_HEADS, TQ, LANES), F32),
                        pltpu.VMEM((N_HEADS, TQ, CK), F32),
                        pltpu.VMEM((N_HEADS, TQ, 2 * LANES), F32)],
        compiler_params=pltpu.CompilerParams(vmem_limit_bytes=40 * 1024 * 1024),
        name="moba_prompt",
    )(qb, kb16, vb16, kmean)


def _page_copy(cache_ref, buf_ref, sem_ref, page, slot, p):
    window = pl.ds(pl.multiple_of(p * PAGE, PAGE), PAGE)
    return pltpu.make_async_copy(cache_ref.at[page], buf_ref.at[slot, :, window], sem_ref.at[slot])


def _start_pages(pt_ref, seq_idx, slot, n_pages, streams):
    def body(p, carry):
        page = pt_ref[seq_idx, p]
        for cache_ref, buf_ref, sem_ref in streams:
            _page_copy(cache_ref, buf_ref, sem_ref, page, slot, p).start()
        return carry

    lax.fori_loop(0, n_pages, body, 0)


def _wait_pages(slot, n_pages, streams):
    def body(p, carry):
        for cache_ref, buf_ref, sem_ref in streams:
            _page_copy(cache_ref, buf_ref, sem_ref, 0, slot, p).wait()
        return carry

    lax.fori_loop(0, n_pages, body, 0)


def _gather_pipeline(pt_ref, n_pages, streams):
    s = pl.program_id(0)
    slot = s % 2

    @pl.when(s == 0)
    def _():
        _start_pages(pt_ref, s, slot, n_pages, streams)

    @pl.when(s + 1 < pl.num_programs(0))
    def _():
        _start_pages(pt_ref, s + 1, 1 - slot, n_pages, streams)

    _wait_pages(slot, n_pages, streams)
    return slot


def _dec_scores_kernel(pt_ref, q_ref, w_ref, knew_ref, cache_ref, o_ref, kbuf, sem):
    n_pages = pt_ref.shape[1]
    past = n_pages * PAGE
    slot = _gather_pipeline(pt_ref, n_pages, [(cache_ref, kbuf, sem)])
    q = q_ref[...]
    w = w_ref[...]
    qk = q[:, :IDX_DIM]
    for ch in range(past // DEC_CHUNK):
        kc = kbuf[slot, :, ch * DEC_CHUNK:(ch + 1) * DEC_CHUNK].astype(BF16)
        lg = _dot(qk, kc)
        o_ref[:, ch * DEC_CHUNK:(ch + 1) * DEC_CHUNK] = jnp.sum(
            w * jnp.maximum(lg, 0.0), axis=0, keepdims=True)
    knew = knew_ref[...].astype(BF16).astype(F32)
    lg_new = jnp.sum(q.astype(F32) * knew, axis=1, keepdims=True)
    s_new = jnp.sum(w * jnp.maximum(lg_new, 0.0), axis=0, keepdims=True)
    lane = lax.broadcasted_iota(jnp.int32, (1, CK), 1)
    o_ref[:, past:past + CK] = jnp.where(lane == 0, s_new, -jnp.inf)


def _dec_scores(page_table, qi_s, w_s, knew, cache):
    ns, n_pages = page_table.shape
    past = n_pages * PAGE
    width = past + CK
    grid_spec = pltpu.PrefetchScalarGridSpec(
        num_scalar_prefetch=1,
        grid=(ns,),
        in_specs=[pl.BlockSpec((None, N_HEADS, LANES), lambda s, pt: (s, 0, 0)),
                  pl.BlockSpec((None, N_HEADS, 1), lambda s, pt: (s, 0, 0)),
                  pl.BlockSpec((None, 1, LANES), lambda s, pt: (s, 0, 0)),
                  pl.BlockSpec(memory_space=pl.ANY)],
        out_specs=pl.BlockSpec((None, 1, width), lambda s, pt: (s, 0, 0)),
        scratch_shapes=[pltpu.VMEM((2, IDX_DIM, past), F32),
                        pltpu.SemaphoreType.DMA((2,))])
    return pl.pallas_call(
        _dec_scores_kernel,
        grid_spec=grid_spec,
        out_shape=jax.ShapeDtypeStruct((ns, 1, width), F32),
        compiler_params=pltpu.CompilerParams(dimension_semantics=("arbitrary",),
                                             vmem_limit_bytes=32 * 1024 * 1024),
        name="dec_scores",
    )(page_table, qi_s, w_s, knew, cache)


def _dec_select_kernel(s_ref, bias_ref, sc_ref):
    rows, width = s_ref.shape
    nch = width // CK
    for c in range(nch):
        sc_ref[c] = s_ref[:, c * CK:(c + 1) * CK]
    for c in range(nch, sc_ref.shape[0]):
        sc_ref[c] = jnp.full((rows, CK), -jnp.inf, F32)
    nvalid = jnp.full((rows, 1), float(width - CK + 1), F32)
    kcol = jnp.minimum(nvalid, float(DSA_TOPK))
    t = _topk_threshold(sc_ref, nch, kcol, nvalid)
    tb = jnp.broadcast_to(t, (rows, CK))
    for c in range(nch):
        bias_ref[:, c * CK:(c + 1) * CK] = jnp.where(sc_ref[c] > tb, 0.0, NEG)


def _dec_select(scores):
    rows, width = scores.shape
    return pl.pallas_call(
        _dec_select_kernel,
        grid=(1,),
        in_specs=[pl.BlockSpec((rows, width), lambda i: (0, 0))],
        out_specs=pl.BlockSpec((rows, width), lambda i: (0, 0)),
        out_shape=jax.ShapeDtypeStruct((rows, width), F32),
        scratch_shapes=[pltpu.VMEM((2 * ((width // CK + 1) // 2), rows, CK), F32)],
        compiler_params=pltpu.CompilerParams(vmem_limit_bytes=40 * 1024 * 1024),
        name="dec_select",
    )(scores)


def _decode_scores(q, kbuf, slot, past, s_ref):
    for ch in range(past // DEC_CHUNK):
        win = slice(ch * DEC_CHUNK, (ch + 1) * DEC_CHUNK)
        s_ref[:, win] = _dot(q, kbuf[slot, :, win].astype(BF16))


def _decode_softmax_pv(q, vbuf, slot, past, s_ref, knew, vnew, bias_new, o_ref):
    n_ch = past // DEC_CHUNK
    s_new = jnp.sum(q.astype(F32) * knew.astype(BF16).astype(F32), axis=1, keepdims=True) + bias_new
    m = s_new
    for ch in range(n_ch):
        win = slice(ch * DEC_CHUNK, (ch + 1) * DEC_CHUNK)
        m = jnp.maximum(m, jnp.max(s_ref[:, win], axis=1, keepdims=True))
    p_new = jnp.exp2(s_new - m)
    l = p_new
    acc = p_new * vnew.astype(BF16).astype(F32)
    for ch in range(n_ch):
        win = slice(ch * DEC_CHUNK, (ch + 1) * DEC_CHUNK)
        p = jnp.exp2(s_ref[:, win] - m)
        l = l + jnp.sum(p, axis=1, keepdims=True)
        acc = acc + _dot_nt(p.astype(BF16), vbuf[slot, :, win].astype(BF16))
    o = acc / l
    head = lax.broadcasted_iota(jnp.int32, o.shape, 0)
    o_ref[...] = jnp.where(head < GROUP, o, pltpu.roll(o, HEAD_DIM, axis=1))


def _dec_dsa_kernel(pt_ref, q_ref, bias_ref, knew_ref, vnew_ref, kc_ref, vc_ref, o_ref,
                    kbuf, vbuf, s_ref, ksem, vsem):
    n_pages = pt_ref.shape[1]
    past = n_pages * PAGE
    slot = _gather_pipeline(pt_ref, n_pages, [(kc_ref, kbuf, ksem), (vc_ref, vbuf, vsem)])
    q = q_ref[...]
    _decode_scores(q, kbuf, slot, past, s_ref)
    for ch in range(past // DEC_CHUNK):
        win = slice(ch * DEC_CHUNK, (ch + 1) * DEC_CHUNK)
        s_ref[:, win] = s_ref[:, win] + bias_ref[:, win]
    _decode_softmax_pv(q, vbuf, slot, past, s_ref, knew_ref[...], vnew_ref[...],
                       bias_ref[:, past:past + 1], o_ref)


def _dec_moba_kernel(pt_ref, q_ref, knew_ref, vnew_ref, kc_ref, vc_ref, o_ref,
                     kbuf, vbuf, s_ref, ksem, vsem):
    n_pages = pt_ref.shape[1]
    past = n_pages * PAGE
    n_blocks = past // MOBA_BLOCK
    slot = _gather_pipeline(pt_ref, n_pages, [(kc_ref, kbuf, ksem), (vc_ref, vbuf, vsem)])
    q = q_ref[...]
    _decode_scores(q, kbuf, slot, past, s_ref)
    lane = lax.broadcasted_iota(jnp.int32, (N_HEADS, LANES), 1)
    gate = jnp.zeros((N_HEADS, LANES), F32)
    for b in range(n_blocks):
        blk_sum = jnp.sum(s_ref[:, b * MOBA_BLOCK:(b + 1) * MOBA_BLOCK], axis=1, keepdims=True)
        gate = jnp.where(lane == b, blk_sum * (1.0 / MOBA_BLOCK), gate)
    sel = _moba_select(gate, n_blocks)
    for b in range(n_blocks):
        win = slice(b * MOBA_BLOCK, (b + 1) * MOBA_BLOCK)
        s_ref[:, win] = s_ref[:, win] + (sel[:, b:b + 1] - 1.0) * (-NEG)
    _decode_softmax_pv(q, vbuf, slot, past, s_ref, knew_ref[...], vnew_ref[...], 0.0, o_ref)


def _dec_attend_call(kernel, name, page_table, q_s, extra, knew, vnew, k_cache, v_cache):
    ns, n_pages = page_table.shape
    past = n_pages * PAGE
    per_seq = lambda shape: pl.BlockSpec((None,) + shape, lambda s, pt: (s, 0, 0))
    in_specs = [per_seq((N_HEADS, LANES))]
    in_specs += [per_seq((1, e.shape[2])) for e in extra]
    in_specs += [per_seq((1, LANES)), per_seq((1, LANES)),
                 pl.BlockSpec(memory_space=pl.ANY), pl.BlockSpec(memory_space=pl.ANY)]
    grid_spec = pltpu.PrefetchScalarGridSpec(
        num_scalar_prefetch=1,
        grid=(ns,),
        in_specs=in_specs,
        out_specs=per_seq((N_HEADS, LANES)),
        scratch_shapes=[pltpu.VMEM((2, LANES, past), F32),
                        pltpu.VMEM((2, LANES, past), F32),
                        pltpu.VMEM((N_HEADS, past), F32),
                        pltpu.SemaphoreType.DMA((2,)), pltpu.SemaphoreType.DMA((2,))])
    return pl.pallas_call(
        kernel,
        grid_spec=grid_spec,
        out_shape=jax.ShapeDtypeStruct((ns, N_HEADS, LANES), F32),
        compiler_params=pltpu.CompilerParams(dimension_semantics=("arbitrary",),
                                             vmem_limit_bytes=48 * 1024 * 1024),
        name=name,
    )(page_table, q_s, *extra, knew, vnew, k_cache, v_cache)


def _heads_from_padded(o):
    return o[:, :, :HEAD_DIM].reshape(o.shape[0], BR_WIDTH)


def _finish_kernel(x_ref, oa_ref, ob_ref, za_ref, zb_ref, ga_ref, gb_ref, ada_ref,
                   wa_ref, wb_ref, wo_ref, gf_ref, y_ref):
    ya = _dot((oa_ref[...] * za_ref[...].astype(F32)).astype(BF16), wa_ref[...])
    yb = _dot((ob_ref[...] * zb_ref[...].astype(F32)).astype(BF16), wb_ref[...])
    m = ga_ref[...].astype(F32) * ya + gb_ref[...].astype(F32) * yb
    gate = ada_ref[...][:, 2 * D_MODEL:]
    xo = x_ref[...] + gate * _dot(m.astype(BF16), wo_ref[...])
    ms = jnp.mean(xo * xo, axis=-1, keepdims=True)
    y_ref[...] = (xo * lax.rsqrt(ms + EPS)) * gf_ref[...]


def _finish(x, oa, ob, za, zb, ga, gb, ada, wa, wb, wo, gf, tm, rows_per_ada):
    rows = x.shape[0]
    tiles_per_ada = rows_per_ada // tm
    a_rows = ada.shape[1]
    row_spec = lambda w: pl.BlockSpec((tm, w), lambda i: (i, 0))
    full = lambda a: pl.BlockSpec(a.shape, lambda i: (0, 0))
    return pl.pallas_call(
        _finish_kernel,
        grid=(rows // tm,),
        in_specs=[row_spec(D_MODEL), row_spec(BR_WIDTH), row_spec(BR_WIDTH),
                  row_spec(BR_WIDTH), row_spec(BR_WIDTH), row_spec(D_MODEL), row_spec(D_MODEL),
                  pl.BlockSpec((None, a_rows, 3 * D_MODEL), lambda i: (i // tiles_per_ada, 0, 0)),
                  full(wa), full(wb), full(wo), full(gf)],
        out_specs=row_spec(D_MODEL),
        out_shape=jax.ShapeDtypeStruct((rows, D_MODEL), F32),
        compiler_params=pltpu.CompilerParams(vmem_limit_bytes=40 * 1024 * 1024),
        name="finish",
    )(x, oa, ob, za, zb, ga, gb, ada, wa, wb, wo, gf)


def _rope_tables(pos):
    half = HEAD_DIM // 2
    freqs = ROPE_THETA ** (-jnp.arange(half, dtype=F32) / half)
    ang = pos.astype(F32)[:, None] * freqs[None, :]
    cos = jnp.cos(ang)
    sin = jnp.sin(ang)
    cos_t = jnp.concatenate([cos, cos, cos, cos], axis=1)
    sin_t = jnp.concatenate([-sin, sin, -sin, sin], axis=1)
    return cos_t, sin_t


def kernel(x_prompt, x_sample, cache_k_a, cache_v_a, cache_k_idx, cache_k_b, cache_v_b, page_table,
           c_prompt, c_sample, w_ada, b_ada, w_in, w_br_a, w_br_b, w_o, g_final):
    n_batch, seq, _ = x_prompt.shape
    ns, dec_seq, _ = x_sample.shape
    depth = w_in.shape[0]
    assert depth == 1 and dec_seq == 1 and seq % MOBA_BLOCK == 0
    n_pages = page_table.shape[1]
    past = n_pages * PAGE
    n_pool = cache_k_a.shape[1]
    tm = MOBA_BLOCK

    pad_rows = (-(n_batch + ns)) % 8
    c_all = jnp.concatenate([c_prompt, c_sample, jnp.zeros((pad_rows, D_MODEL), F32)], axis=0)
    ada = _ada(c_all, w_ada[0], b_ada[0][None, :])
    ada_p = ada[:n_batch].reshape(n_batch, 1, 3 * D_MODEL)
    ada_s = ada[n_batch:n_batch + ns].reshape(1, ns, 3 * D_MODEL)

    w_packed = jnp.concatenate(
        [w_in[0][:, :_ORIG_SPLIT], jnp.zeros((D_MODEL, _PAD_COLS), F32), w_in[0][:, _ORIG_SPLIT:]],
        axis=1).astype(BF16)
    cos_p, sin_p = _rope_tables(jnp.arange(seq, dtype=jnp.int32))
    cos_s, sin_s = _rope_tables(jnp.full((ns,), past, dtype=jnp.int32))

    (qa, qi, qb, ka32, va32, kiw32, kb32, vb32, ka16, va16, kiw16, kb16, vb16,
     za, zb, ga, gb, kmean) = _prepare(x_prompt.reshape(n_batch * seq, D_MODEL), ada_p,
                                       cos_p, sin_p, w_packed, tm, seq)
    oa = _dsa_prompt(qi, qa, kiw32, kiw16, ka16, va16, n_batch, seq)
    n_blk = seq // MOBA_BLOCK
    km_pad = jnp.pad(kmean.reshape(n_batch, n_blk, LANES), ((0, 0), (0, LANES - n_blk), (0, 0)))
    ob = _moba_prompt(qb, kb16, vb16, km_pad, n_batch, seq)
    wa = w_br_a[0].astype(BF16)
    wb = w_br_b[0].astype(BF16)
    wo = w_o[0].astype(BF16)
    gf = g_final[None, :]
    y_p = _finish(x_prompt.reshape(n_batch * seq, D_MODEL), oa, ob, za, zb, ga, gb, ada_p,
                  wa, wb, wo, gf, tm, seq)

    (qa_s, qi_s, qb_s, ka32_s, va32_s, kiw32_s, kb32_s, vb32_s, _, _, _, _, _,
     za_s, zb_s, ga_s, gb_s, _) = _prepare(x_sample.reshape(ns, D_MODEL), ada_s,
                                           cos_s, sin_s, w_packed, ns, ns)
    per_seq_heads = lambda q: jnp.transpose(q, (1, 0, 2))
    row3 = lambda a: a.reshape(ns, 1, LANES)
    w_s = kiw32_s[:, IDX_DIM:IDX_DIM + N_HEADS].reshape(ns, N_HEADS, 1)
    kv_pages = lambda c: jnp.transpose(c[0], (0, 2, 3, 1)).reshape(n_pool, LANES, PAGE)
    idx_pages = jnp.transpose(cache_k_idx[0], (0, 2, 1))
    scores = _dec_scores(page_table, per_seq_heads(qi_s), w_s, row3(kiw32_s), idx_pages)
    bias = _dec_select(scores.reshape(ns, past + CK)).reshape(ns, 1, past + CK)
    oa_s = _dec_attend_call(_dec_dsa_kernel, "dec_dsa", page_table, per_seq_heads(qa_s), [bias],
                            row3(ka32_s), row3(va32_s), kv_pages(cache_k_a), kv_pages(cache_v_a))
    ob_s = _dec_attend_call(_dec_moba_kernel, "dec_moba", page_table, per_seq_heads(qb_s), [],
                            row3(kb32_s), row3(vb32_s), kv_pages(cache_k_b), kv_pages(cache_v_b))
    y_s = _finish(x_sample.reshape(ns, D_MODEL), _heads_from_padded(oa_s), _heads_from_padded(ob_s),
                  za_s, zb_s, ga_s, gb_s, ada_s, wa, wb, wo, gf, ns, ns)

    kv = lambda a, n, t: a.reshape(1, n, t, KV_HEADS, HEAD_DIM)
    ki = lambda a, n, t: a[:, :IDX_DIM].reshape(1, n, t, IDX_DIM)
    return (y_p.reshape(n_batch, seq, D_MODEL), y_s.reshape(ns, 1, D_MODEL),
            kv(ka32, n_batch, seq), kv(va32, n_batch, seq), ki(kiw32, n_batch, seq),
            kv(kb32, n_batch, seq), kv(vb32, n_batch, seq),
            kv(ka32_s, ns, 1), kv(va32_s, ns, 1), ki(kiw32_s, ns, 1),
            kv(kb32_s, ns, 1), kv(vb32_s, ns, 1))
```

```python
import functools

import jax
import jax.numpy as jnp
from jax import lax
from jax.experimental import pallas as pl
from jax.experimental.pallas import tpu as pltpu

F32 = jnp.float32
BF16 = jnp.bfloat16

D_MODEL = 1024
HEAD_DIM = 64
N_HEADS = 8
KV_HEADS = 2
GROUP = N_HEADS // KV_HEADS
IDX_DIM = 64
BR_WIDTH = N_HEADS * HEAD_DIM
DSA_TOPK = 256
MOBA_BLOCK = 256
MOBA_TOPK = 3
PAGE = 128
ROPE_THETA = 10000.0
EPS = 1e-6
LANES = 128
CK = 256
TQ = 256
NEG = -1e30
DEC_CHUNK = 1024

_ORIG_SPLIT = 1864
_PAD_COLS = 56
C_QA, C_KA, C_VA, C_ZA, C_QI, C_KIW, C_QB, C_KB, C_VB, C_ZB, C_GA, C_GB, C_END = (
    0, 512, 640, 768, 1280, 1792, 1920, 2432, 2560, 2688, 3200, 4224, 5248)

LOG2E = 1.4426950408889634
_ATT_WIDTHS = (LANES, 2 * LANES, LANES, 2 * LANES, 2 * LANES)

_NT = (((1,), (1,)), ((), ()))


def _dot_nt(a, b):
    return lax.dot_general(a, b, _NT, preferred_element_type=F32)


def _dot(a, b):
    return jnp.dot(a, b, preferred_element_type=F32)


def _ada_kernel(c_ref, w_ref, b_ref, o_ref):
    o_ref[...] = _dot(c_ref[...].astype(BF16), w_ref[...].astype(BF16)) + b_ref[...]


def _ada(c_all, w_ada, b_ada):
    rows = c_all.shape[0]
    return pl.pallas_call(
        _ada_kernel,
        grid=(3,),
        in_specs=[pl.BlockSpec((rows, D_MODEL), lambda j: (0, 0)),
                  pl.BlockSpec((D_MODEL, D_MODEL), lambda j: (0, j)),
                  pl.BlockSpec((1, D_MODEL), lambda j: (0, j))],
        out_specs=pl.BlockSpec((rows, D_MODEL), lambda j: (0, j)),
        out_shape=jax.ShapeDtypeStruct((rows, 3 * D_MODEL), F32),
        name="ada",
    )(c_all, w_ada, b_ada)


def _rope(x, cos, sin_signed):
    w = x.shape[1]
    reps = w // LANES
    lane = lax.broadcasted_iota(jnp.int32, x.shape, 1)
    upper = (lane % HEAD_DIM) >= (HEAD_DIM // 2)
    from_below = pltpu.roll(x, HEAD_DIM // 2, axis=1)
    from_above = pltpu.roll(x, w - HEAD_DIM // 2, axis=1)
    partner = jnp.where(upper, from_below, from_above)
    if reps > 1:
        cos = jnp.concatenate([cos] * reps, axis=1)
        sin_signed = jnp.concatenate([sin_signed] * reps, axis=1)
    return x * cos + partner * sin_signed


def _store_padded_heads(q, out_ref, low_for_head):
    lane = lax.broadcasted_iota(jnp.int32, (q.shape[0], LANES), 1)
    is_low = lane < HEAD_DIM
    for h in range(N_HEADS):
        col = q[:, (h // 2) * LANES:(h // 2 + 1) * LANES]
        want_low = low_for_head(h)
        if (h % 2 == 0) != want_low:
            col = pltpu.roll(col, HEAD_DIM, axis=1)
        keep = is_low if want_low else jnp.logical_not(is_low)
        out_ref[h] = jnp.where(keep, col, 0.0).astype(BF16)


def _prep_kernel(x_ref, ada_ref, cos_ref, sin_ref, w_ref,
                 qa_ref, qi_ref, qb_ref,
                 ka32_ref, va32_ref, kiw32_ref, kb32_ref, vb32_ref,
                 ka16_ref, va16_ref, kiw16_ref, kb16_ref, vb16_ref,
                 za_ref, zb_ref, ga_ref, gb_ref, km_ref, *, tiles_per_seq):
    x = x_ref[...]
    ada = ada_ref[...]
    shift = ada[:, :D_MODEL]
    scale = ada[:, D_MODEL:2 * D_MODEL]
    ms = jnp.mean(x * x, axis=-1, keepdims=True)
    h = (x * lax.rsqrt(ms + EPS)) * (1.0 + scale) + shift
    hb = h.astype(BF16)
    cos = cos_ref[...]
    sin = sin_ref[...]

    def proj(a, b):
        return _dot(hb, w_ref[:, a:b])

    kv_low = lambda hh: hh < GROUP
    qscale = HEAD_DIM ** -0.5 * LOG2E
    _store_padded_heads(_rope(proj(C_QA, C_KA), cos, sin) * qscale, qa_ref, kv_low)
    _store_padded_heads(_rope(proj(C_QI, C_KIW), cos, sin) * (IDX_DIM ** -0.5), qi_ref,
                        lambda hh: True)
    _store_padded_heads(_rope(proj(C_QB, C_KB), cos, sin) * qscale, qb_ref, kv_low)

    ka = _rope(proj(C_KA, C_VA), cos, sin)
    ka32_ref[...] = ka
    ka16_ref[...] = ka.astype(BF16)
    va = proj(C_VA, C_ZA)
    va32_ref[...] = va
    ones = jnp.ones(va.shape, BF16)
    va16_ref[...] = jnp.concatenate([va.astype(BF16), ones], axis=1)

    kiw = proj(C_KIW, C_QB)
    lane = lax.broadcasted_iota(jnp.int32, kiw.shape, 1)
    kiw = jnp.where(lane < IDX_DIM, _rope(kiw, cos, sin), kiw * (N_HEADS ** -0.5))
    kiw32_ref[...] = kiw
    kiw16_ref[...] = kiw.astype(BF16)

    kb = _rope(proj(C_KB, C_VB), cos, sin)
    kb32_ref[...] = kb
    own_block = pl.program_id(0) % tiles_per_seq
    block_code = jnp.where(lane == own_block, 1.0, 0.0).astype(BF16)
    kb16_ref[...] = jnp.concatenate([kb.astype(BF16), block_code], axis=1)
    km_ref[0] = jnp.sum(kb, axis=0, keepdims=True) * (1.0 / MOBA_BLOCK)
    vb = proj(C_VB, C_ZB)
    vb32_ref[...] = vb
    vb16_ref[...] = jnp.concatenate([vb.astype(BF16), ones], axis=1)

    za_ref[...] = jax.nn.silu(proj(C_ZA, C_QI)).astype(BF16)
    zb_ref[...] = jax.nn.silu(proj(C_ZB, C_GA)).astype(BF16)
    ga_ref[...] = jax.nn.sigmoid(proj(C_GA, C_GB)).astype(BF16)
    gb_ref[...] = jax.nn.sigmoid(proj(C_GB, C_END)).astype(BF16)


def _prepare(x, ada, cos, sin, w_packed, tm, rows_per_ada):
    rows = x.shape[0]
    nt = rows // tm
    tab_tiles = cos.shape[0] // tm
    tiles_per_ada = rows_per_ada // tm
    a_rows = ada.shape[1]
    row_spec = lambda w: pl.BlockSpec((tm, w), lambda i: (i, 0))
    head_spec = pl.BlockSpec((N_HEADS, tm, LANES), lambda i: (0, i, 0))
    sd = jax.ShapeDtypeStruct
    out_shape = (
        [sd((N_HEADS, rows, LANES), BF16)] * 3
        + [sd((rows, LANES), F32)] * 5
        + [sd((rows, w), BF16) for w in _ATT_WIDTHS]
        + [sd((rows, BR_WIDTH), BF16)] * 2
        + [sd((rows, D_MODEL), BF16)] * 2
        + [sd((nt, 1, LANES), F32)])
    out_specs = (
        [head_spec] * 3 + [row_spec(LANES)] * 5 + [row_spec(w) for w in _ATT_WIDTHS]
        + [row_spec(BR_WIDTH)] * 2
        + [row_spec(D_MODEL)] * 2 + [pl.BlockSpec((1, 1, LANES), lambda i: (i, 0, 0))])
    return pl.pallas_call(
        functools.partial(_prep_kernel, tiles_per_seq=tiles_per_ada),
        grid=(nt,),
        in_specs=[row_spec(D_MODEL),
                  pl.BlockSpec((None, a_rows, 3 * D_MODEL), lambda i: (i // tiles_per_ada, 0, 0)),
                  pl.BlockSpec((tm, LANES), lambda i: (i % tab_tiles, 0)),
                  pl.BlockSpec((tm, LANES), lambda i: (i % tab_tiles, 0)),
                  pl.BlockSpec((D_MODEL, C_END), lambda i: (0, 0))],
        out_specs=out_specs,
        out_shape=out_shape,
        compiler_params=pltpu.CompilerParams(vmem_limit_bytes=56 * 1024 * 1024),
        name="prepare",
    )(x, ada, cos, sin, w_packed)


def _topk_threshold(sc_ref, nc, kcol, nvalid):
    rows = sc_ref.shape[1]
    n_pairs = (nc + 1) // 2

    band = 128
    bands = [slice(r, r + band) for r in range(0, rows, band)]
    halves = (slice(0, LANES), slice(LANES, CK))

    def count_gt(tcol):
        parts = []
        for rs in bands:
            tb = jnp.broadcast_to(tcol[rs], (band, LANES))

            def body(i, acc, rs=rs, tb=tb):
                for c in (2 * i, 2 * i + 1):
                    for ls in halves:
                        acc = acc + jnp.where(sc_ref[c, rs, ls] > tb, 1.0, 0.0)
                return acc

            acc = lax.fori_loop(0, n_pairs, body, jnp.zeros((band, LANES), F32))
            parts.append(jnp.sum(acc, axis=1, keepdims=True))
        return jnp.concatenate(parts, axis=0)

    pos_parts, nonneg_parts, neg_parts = [], [], []
    for rs in bands:
        def zero_stats(c, carry, rs=rs):
            n_pos, n_nonneg, neg_max = carry
            for ls in halves:
                s = sc_ref[c, rs, ls]
                n_pos = n_pos + jnp.where(s > 0.0, 1.0, 0.0)
                n_nonneg = n_nonneg + jnp.where(s >= 0.0, 1.0, 0.0)
                neg_max = jnp.maximum(neg_max, jnp.where(s < 0.0, s, -jnp.inf))
            return n_pos, n_nonneg, neg_max

        zl = jnp.zeros((band, LANES), F32)
        n_pos, n_nonneg, neg_max = lax.fori_loop(
            0, nc, zero_stats, (zl, zl, jnp.full((band, LANES), -jnp.inf, F32)))
        pos_parts.append(jnp.sum(n_pos, axis=1, keepdims=True))
        nonneg_parts.append(jnp.sum(n_nonneg, axis=1, keepdims=True))
        neg_parts.append(jnp.max(neg_max, axis=1, keepdims=True))
    c_pos = jnp.concatenate(pos_parts, axis=0)
    c_nonneg = jnp.concatenate(nonneg_parts, axis=0)
    neg_max = jnp.concatenate(neg_parts, axis=0)

    max_parts, min_parts = [], []
    for rs in bands:
        def extremes(c, carry, rs=rs):
            vmax, vmin = carry
            for ls in halves:
                s = sc_ref[c, rs, ls]
                vmax = jnp.maximum(vmax, s)
                vmin = jnp.minimum(vmin, jnp.where(s > -jnp.inf, s, jnp.inf))
            return vmax, vmin

        vmax, vmin = lax.fori_loop(0, nc, extremes, (jnp.full((band, LANES), -jnp.inf, F32),
                                                     jnp.full((band, LANES), jnp.inf, F32)))
        max_parts.append(jnp.max(vmax, axis=1, keepdims=True))
        min_parts.append(jnp.min(vmin, axis=1, keepdims=True))
    row_max = jnp.concatenate(max_parts, axis=0)
    row_min = jnp.concatenate(min_parts, axis=0)

    lo0 = row_min - 1.0 - jnp.abs(row_min) * (2.0 ** -10)
    below_zero = jnp.where(neg_max > -jnp.inf, neg_max, lo0)
    zeros = jnp.zeros_like(row_max)
    take_all = nvalid <= kcol
    open_row = jnp.logical_not(take_all)
    hit_zero = jnp.logical_and(open_row, c_pos == kcol)
    tie_zero = jnp.logical_and(open_row, jnp.logical_and(c_pos < kcol, c_nonneg >= kcol))
    above = jnp.logical_and(open_row, c_pos > kcol)
    below = jnp.logical_and(open_row, c_nonneg < kcol)
    searching = jnp.logical_or(above, below)
    lo_init = jnp.where(jnp.logical_or(above, hit_zero), 0.0, jnp.where(tie_zero, below_zero, lo0))
    hi_init = jnp.where(above, row_max, jnp.where(below, below_zero, 0.0))
    state0 = jnp.where(searching, 0.0, jnp.where(tie_zero, 2.0, 1.0))
    n_live0 = jnp.sum(jnp.where(searching, 1.0, 0.0)).astype(jnp.int32)
    init = (lo_init, hi_init, state0, n_live0, jnp.int32(0))

    def cond(st):
        return jnp.logical_and(st[3] > 0, st[4] < 400)

    def step(st):
        lo, hi, state, _, it = st
        mid = lo + 0.5 * (hi - lo)
        c = count_gt(mid)
        live = state < 0.5
        hit = c == kcol
        spent = jnp.logical_or(mid <= lo, mid >= hi)
        finish_hit = jnp.logical_and(live, hit)
        finish_tie = jnp.logical_and(live, jnp.logical_and(spent, jnp.logical_not(hit)))
        keep_going = jnp.logical_and(live, jnp.logical_not(finish_tie))
        lo = jnp.where(jnp.logical_and(keep_going, c >= kcol), mid, lo)
        hi = jnp.where(jnp.logical_and(keep_going, c < kcol), mid, hi)
        state = jnp.where(finish_hit, 1.0, jnp.where(finish_tie, 2.0, state))
        n_live = jnp.sum(jnp.where(state < 0.5, 1.0, 0.0)).astype(jnp.int32)
        return (lo, hi, state, n_live, it + 1)

    def body(st):
        return step(step(st))

    lo, hi, state, _, _ = lax.while_loop(cond, body, init)
    t = lo
    tie = jnp.where(state > 1.5, 1.0, 0.0)

    @pl.when(jnp.sum(tie) > 0.5)
    def _():
        need = kcol - count_gt(hi)
        ri = lax.broadcasted_iota(jnp.int32, (CK, CK), 0)
        ci = lax.broadcasted_iota(jnp.int32, (CK, CK), 1)
        tri = jnp.where(ri <= ci, 1.0, 0.0).astype(BF16)
        hib = jnp.broadcast_to(hi, (rows, CK))
        tieb = jnp.broadcast_to(tie, (rows, CK)) > 0.5

        def fix(c, before):
            s = sc_ref[c]
            eq = jnp.where(jnp.logical_and(s == hib, tieb), 1.0, 0.0)
            rank = before + _dot(eq.astype(BF16), tri)
            drop = jnp.logical_and(eq > 0.5, rank > need)
            sc_ref[c] = jnp.where(drop, -jnp.inf, s)
            return before + jnp.sum(eq, axis=1, keepdims=True)

        lax.fori_loop(0, nc, fix, jnp.zeros_like(hi))

    return t


def _scores(q_ref, kc):
    q = q_ref[...].reshape(N_HEADS * TQ, q_ref.shape[2])
    return _dot_nt(q, kc).reshape(N_HEADS, TQ, CK)


def _track_max(s, mrun_ref):
    mrun_ref[...] = jnp.maximum(mrun_ref[...], jnp.maximum(s[:, :, :LANES], s[:, :, LANES:]))


def _finish_max(mrun_ref, mb_ref):
    m = jnp.max(mrun_ref[...], axis=2, keepdims=True)
    mb_ref[...] = jnp.broadcast_to(m, mb_ref.shape)


def _accumulate_pv(s, vc, mb_ref, acc_ref):
    p = jnp.exp2(s - mb_ref[...]).astype(BF16).reshape(N_HEADS * TQ, CK)
    acc_ref[...] = acc_ref[...] + _dot(p, vc).reshape(acc_ref.shape)


def _two_pass_attention(n_chunks, q_ref, k_ref, v_ref, bias_of_chunk, mrun_ref, mb_ref, acc_ref):
    mrun_ref[...] = jnp.full(mrun_ref.shape, NEG, F32)
    acc_ref[...] = jnp.zeros(acc_ref.shape, F32)
    n_pairs = (n_chunks + 1) // 2

    def max_pair(i, carry):
        for j in range(2):
            c = 2 * i + j
            kc = k_ref[pl.ds(pl.multiple_of(c * CK, CK), CK), :]
            _track_max(_scores(q_ref, kc) + bias_of_chunk(c)[None], mrun_ref)
        return carry

    lax.fori_loop(0, n_pairs, max_pair, 0)
    _finish_max(mrun_ref, mb_ref)

    def attend_pair(i, carry):
        for j in range(2):
            c = 2 * i + j
            off = pl.multiple_of(c * CK, CK)
            _accumulate_pv(_scores(q_ref, k_ref[pl.ds(off, CK), :]) + bias_of_chunk(c)[None],
                           v_ref[pl.ds(off, CK), :], mb_ref, acc_ref)
        return carry

    lax.fori_loop(0, n_pairs, attend_pair, 0)


def _write_heads(o_ref, acc_ref):
    lane = lax.broadcasted_iota(jnp.int32, (o_ref.shape[0], LANES), 1)
    is_low = lane < HEAD_DIM
    for pair in range(N_HEADS // 2):
        halves = []
        for h in (2 * pair, 2 * pair + 1):
            acc = acc_ref[h]
            o = acc[:, :LANES] / acc[:, LANES:]
            in_low = h < GROUP
            want_low = h % 2 == 0
            if in_low != want_low:
                o = pltpu.roll(o, HEAD_DIM, axis=1)
            halves.append(o)
        o_ref[:, pair * LANES:(pair + 1) * LANES] = jnp.where(is_low, halves[0], halves[1])


def _dsa_prompt_kernel(qi_ref, qa_ref, wt_ref, kiw_ref, ka_ref, va_ref, oa_ref,
                       sc_ref, wb_ref, mrun_ref, mb_ref, acc_ref):
    qi = pl.program_id(1)
    nc = (qi * TQ + TQ + CK - 1) // CK
    qpos = qi * TQ + lax.broadcasted_iota(jnp.int32, (TQ, CK), 0)
    lane = lax.broadcasted_iota(jnp.int32, (TQ, CK), 1)

    wt = wt_ref[...]
    for h in range(N_HEADS):
        wb_ref[h] = jnp.broadcast_to(wt[:, IDX_DIM + h:IDX_DIM + h + 1], (TQ, CK))

    def score_pair(i, carry):
        for c in (2 * i, 2 * i + 1):
            kc = kiw_ref[pl.ds(pl.multiple_of(c * CK, CK), CK), :]
            acc = jnp.sum(wb_ref[...] * jnp.maximum(_scores(qi_ref, kc), 0.0), axis=0)
            sc_ref[c] = jnp.where((c * CK + lane) <= qpos, acc, -jnp.inf)
        return carry

    lax.fori_loop(0, (nc + 1) // 2, score_pair, 0)
    nvalid = (qpos[:, :1] + 1).astype(F32)
    kcol = jnp.minimum(nvalid, float(DSA_TOPK))
    t = _topk_threshold(sc_ref, nc, kcol, nvalid)

    tb = jnp.broadcast_to(t, (TQ, CK))

    def to_bias(c, carry):
        sc_ref[c] = jnp.where(sc_ref[c] > tb, 0.0, NEG)
        return carry

    lax.fori_loop(0, nc, to_bias, 0)

    @pl.when(nc % 2 == 1)
    def _():
        sc_ref[nc] = jnp.full((TQ, CK), NEG, F32)

    _two_pass_attention(nc, qa_ref, ka_ref, va_ref, lambda c: sc_ref[c], mrun_ref, mb_ref, acc_ref)
    _write_heads(oa_ref, acc_ref)


def _dsa_prompt(qi, qa, kiw32, kiw16, ka16, va16, n_batch, seq):
    nq = seq // TQ
    head_spec = pl.BlockSpec((N_HEADS, TQ, LANES), lambda n, i: (0, n * nq + i, 0))
    seq_spec = pl.BlockSpec((seq, LANES), lambda n, i: (n, 0))
    return pl.pallas_call(
        _dsa_prompt_kernel,
        grid=(n_batch, nq),
        in_specs=[head_spec, head_spec,
                  pl.BlockSpec((TQ, LANES), lambda n, i: (n * nq + i, 0)),
                  seq_spec, seq_spec,
                  pl.BlockSpec((seq, 2 * LANES), lambda n, i: (n, 0))],
        out_specs=pl.BlockSpec((TQ, BR_WIDTH), lambda n, i: (n * nq + i, 0)),
        out_shape=jax.ShapeDtypeStruct((n_batch * seq, BR_WIDTH), F32),
        scratch_shapes=[pltpu.VMEM((seq // CK, TQ, CK), F32),
                        pltpu.VMEM((N_HEADS, TQ, CK), F32),
                        pltpu.VMEM((N_HEADS, TQ, LANES), F32),
                        pltpu.VMEM((N_HEADS, TQ, CK), F32),
                        pltpu.VMEM((N_HEADS, TQ, 2 * LANES), F32)],
        compiler_params=pltpu.CompilerParams(vmem_limit_bytes=56 * 1024 * 1024),
        name="dsa_prompt",
    )(qi, qa, kiw32, kiw16, ka16, va16)


def _moba_select(gate, n_past):
    blk = lax.broadcasted_iota(jnp.int32, gate.shape, 1)
    blkf = blk.astype(F32)
    past = blk < n_past
    g = jnp.where(past, gate, -jnp.inf)
    sel = jnp.zeros(gate.shape, F32)
    for _ in range(MOBA_TOPK):
        top = jnp.max(g, axis=1, keepdims=True)
        first = jnp.min(jnp.where(g == top, blkf, float(LANES)), axis=1, keepdims=True)
        pick = blkf == first
        sel = jnp.where(pick, 1.0, sel)
        g = jnp.where(pick, -jnp.inf, g)
    return jnp.where(past, sel, 0.0)


def _moba_prompt_kernel(qb_ref, kb_ref, vb_ref, km_ref, ob_ref, q2_ref, mrun_ref, mb_ref, acc_ref):
    qi = pl.program_id(1)
    own = (qi * TQ) // MOBA_BLOCK
    km = km_ref[...].astype(BF16)
    blk = lax.broadcasted_iota(jnp.int32, (TQ, LANES), 1)
    for h in range(N_HEADS):
        q = qb_ref[h]
        sel = _moba_select(_dot_nt(q, km), own)
        allowed = jnp.logical_or(sel > 0.5, blk >= own)
        q2_ref[h] = jnp.concatenate([q, jnp.where(allowed, 0.0, NEG).astype(BF16)], axis=1)

    qpos = qi * TQ + lax.broadcasted_iota(jnp.int32, (TQ, CK), 0)
    lane = lax.broadcasted_iota(jnp.int32, (TQ, CK), 1)
    causal_bias = jnp.where((own * CK + lane) <= qpos, 0.0, NEG)

    def bias_of_block(b):
        return jnp.where(b < own, 0.0, jnp.where(b == own, causal_bias, NEG))

    _two_pass_attention(own + 1, q2_ref, kb_ref, vb_ref, bias_of_block, mrun_ref, mb_ref, acc_ref)
    _write_heads(ob_ref, acc_ref)


def _moba_prompt(qb, kb16, vb16, kmean, n_batch, seq):
    nq = seq // TQ
    head_spec = pl.BlockSpec((N_HEADS, TQ, LANES), lambda n, i: (0, n * nq + i, 0))
    seq_spec = pl.BlockSpec((seq, 2 * LANES), lambda n, i: (n, 0))
    return pl.pallas_call(
        _moba_prompt_kernel,
        grid=(n_batch, nq),
        in_specs=[head_spec, seq_spec, seq_spec,
                  pl.BlockSpec((None, LANES, LANES), lambda n, i: (n, 0, 0))],
        out_specs=pl.BlockSpec((TQ, BR_WIDTH), lambda n, i: (n * nq + i, 0)),
        out_shape=jax.ShapeDtypeStruct((n_batch * seq, BR_WIDTH), F32),
        scratch_shapes=[pltpu.VMEM((N_HEADS, TQ, 2 * LANES), BF16),
                        pltpu.VMEM((N_HEADS, TQ, LANES), F32),
                        pltpu.VMEM((N_HEADS, TQ, CK), F32),
                        pltpu.VMEM((N_HEADS, TQ, 2 * LANES), F32)],
        compiler_params=pltpu.CompilerParams(vmem_limit_bytes=40 * 1024 * 1024),
        name="moba_prompt",
    )(qb, kb16, vb16, kmean)


def _page_copy(cache_ref, buf_ref, sem_ref, page, slot, p):
    window = pl.ds(pl.multiple_of(p * PAGE, PAGE), PAGE)
    return pltpu.make_async_copy(cache_ref.at[page], buf_ref.at[slot, :, window], sem_ref.at[slot])


def _start_pages(pt_ref, seq_idx, slot, n_pages, streams):
    def body(p, carry):
        page = pt_ref[seq_idx, p]
        for cache_ref, buf_ref, sem_ref in streams:
            _page_copy(cache_ref, buf_ref, sem_ref, page, slot, p).start()
        return carry

    lax.fori_loop(0, n_pages, body, 0)


def _wait_pages(slot, n_pages, streams):
    def body(p, carry):
        for cache_ref, buf_ref, sem_ref in streams:
            _page_copy(cache_ref, buf_ref, sem_ref, 0, slot, p).wait()
        return carry

    lax.fori_loop(0, n_pages, body, 0)


def _gather_pipeline(pt_ref, n_pages, streams):
    s = pl.program_id(0)
    slot = s % 2

    @pl.when(s == 0)
    def _():
        _start_pages(pt_ref, s, slot, n_pages, streams)

    @pl.when(s + 1 < pl.num_programs(0))
    def _():
        _start_pages(pt_ref, s + 1, 1 - slot, n_pages, streams)

    _wait_pages(slot, n_pages, streams)
    return slot


def _dec_scores_kernel(pt_ref, q_ref, w_ref, knew_ref, cache_ref, o_ref, kbuf, sem):
    n_pages = pt_ref.shape[1]
    past = n_pages * PAGE
    slot = _gather_pipeline(pt_ref, n_pages, [(cache_ref, kbuf, sem)])
    q = q_ref[...]
    w = w_ref[...]
    qk = q[:, :IDX_DIM]
    for ch in range(past // DEC_CHUNK):
        kc = kbuf[slot, :, ch * DEC_CHUNK:(ch + 1) * DEC_CHUNK].astype(BF16)
        lg = _dot(qk, kc)
        o_ref[:, ch * DEC_CHUNK:(ch + 1) * DEC_CHUNK] = jnp.sum(
            w * jnp.maximum(lg, 0.0), axis=0, keepdims=True)
    knew = knew_ref[...].astype(BF16).astype(F32)
    lg_new = jnp.sum(q.astype(F32) * knew, axis=1, keepdims=True)
    s_new = jnp.sum(w * jnp.maximum(lg_new, 0.0), axis=0, keepdims=True)
    lane = lax.broadcasted_iota(jnp.int32, (1, CK), 1)
    o_ref[:, past:past + CK] = jnp.where(lane == 0, s_new, -jnp.inf)


def _dec_scores(page_table, qi_s, w_s, knew, cache):
    ns, n_pages = page_table.shape
    past = n_pages * PAGE
    width = past + CK
    grid_spec = pltpu.PrefetchScalarGridSpec(
        num_scalar_prefetch=1,
        grid=(ns,),
        in_specs=[pl.BlockSpec((None, N_HEADS, LANES), lambda s, pt: (s, 0, 0)),
                  pl.BlockSpec((None, N_HEADS, 1), lambda s, pt: (s, 0, 0)),
                  pl.BlockSpec((None, 1, LANES), lambda s, pt: (s, 0, 0)),
                  pl.BlockSpec(memory_space=pl.ANY)],
        out_specs=pl.BlockSpec((None, 1, width), lambda s, pt: (s, 0, 0)),
        scratch_shapes=[pltpu.VMEM((2, IDX_DIM, past), F32),
                        pltpu.SemaphoreType.DMA((2,))])
    return pl.pallas_call(
        _dec_scores_kernel,
        grid_spec=grid_spec,
        out_shape=jax.ShapeDtypeStruct((ns, 1, width), F32),
        compiler_params=pltpu.CompilerParams(dimension_semantics=("arbitrary",),
                                             vmem_limit_bytes=32 * 1024 * 1024),
        name="dec_scores",
    )(page_table, qi_s, w_s, knew, cache)


def _dec_select_kernel(s_ref, bias_ref, sc_ref):
    rows, width = s_ref.shape
    nch = width // CK
    for c in range(nch):
        sc_ref[c] = s_ref[:, c * CK:(c + 1) * CK]
    for c in range(nch, sc_ref.shape[0]):
        sc_ref[c] = jnp.full((rows, CK), -jnp.inf, F32)
    nvalid = jnp.full((rows, 1), float(width - CK + 1), F32)
    kcol = jnp.minimum(nvalid, float(DSA_TOPK))
    t = _topk_threshold(sc_ref, nch, kcol, nvalid)
    tb = jnp.broadcast_to(t, (rows, CK))
    for c in range(nch):
        bias_ref[:, c * CK:(c + 1) * CK] = jnp.where(sc_ref[c] > tb, 0.0, NEG)


def _dec_select(scores):
    rows, width = scores.shape
    return pl.pallas_call(
        _dec_select_kernel,
        grid=(1,),
        in_specs=[pl.BlockSpec((rows, width), lambda i: (0, 0))],
        out_specs=pl.BlockSpec((rows, width), lambda i: (0, 0)),
        out_shape=jax.ShapeDtypeStruct((rows, width), F32),
        scratch_shapes=[pltpu.VMEM((2 * ((width // CK + 1) // 2), rows, CK), F32)],
        compiler_params=pltpu.CompilerParams(vmem_limit_bytes=40 * 1024 * 1024),
        name="dec_select",
    )(scores)


def _decode_scores(q, kbuf, slot, past, s_ref):
    for ch in range(past // DEC_CHUNK):
        win = slice(ch * DEC_CHUNK, (ch + 1) * DEC_CHUNK)
        s_ref[:, win] = _dot(q, kbuf[slot, :, win].astype(BF16))


def _decode_softmax_pv(q, vbuf, slot, past, s_ref, knew, vnew, bias_new, o_ref):
    n_ch = past // DEC_CHUNK
    s_new = jnp.sum(q.astype(F32) * knew.astype(BF16).astype(F32), axis=1, keepdims=True) + bias_new
    m = s_new
    for ch in range(n_ch):
        win = slice(ch * DEC_CHUNK, (ch + 1) * DEC_CHUNK)
        m = jnp.maximum(m, jnp.max(s_ref[:, win], axis=1, keepdims=True))
    p_new = jnp.exp2(s_new - m)
    l = p_new
    acc = p_new * vnew.astype(BF16).astype(F32)
    for ch in range(n_ch):
        win = slice(ch * DEC_CHUNK, (ch + 1) * DEC_CHUNK)
        p = jnp.exp2(s_ref[:, win] - m)
        l = l + jnp.sum(p, axis=1, keepdims=True)
        acc = acc + _dot_nt(p.astype(BF16), vbuf[slot, :, win].astype(BF16))
    o = acc / l
    head = lax.broadcasted_iota(jnp.int32, o.shape, 0)
    o_ref[...] = jnp.where(head < GROUP, o, pltpu.roll(o, HEAD_DIM, axis=1))


def _dec_dsa_kernel(pt_ref, q_ref, bias_ref, knew_ref, vnew_ref, kc_ref, vc_ref, o_ref,
                    kbuf, vbuf, s_ref, ksem, vsem):
    n_pages = pt_ref.shape[1]
    past = n_pages * PAGE
    slot = _gather_pipeline(pt_ref, n_pages, [(kc_ref, kbuf, ksem), (vc_ref, vbuf, vsem)])
    q = q_ref[...]
    _decode_scores(q, kbuf, slot, past, s_ref)
    for ch in range(past // DEC_CHUNK):
        win = slice(ch * DEC_CHUNK, (ch + 1) * DEC_CHUNK)
        s_ref[:, win] = s_ref[:, win] + bias_ref[:, win]
    _decode_softmax_pv(q, vbuf, slot, past, s_ref, knew_ref[...], vnew_ref[...],
                       bias_ref[:, past:past + 1], o_ref)


def _dec_moba_kernel(pt_ref, q_ref, knew_ref, vnew_ref, kc_ref, vc_ref, o_ref,
                     kbuf, vbuf, s_ref, ksem, vsem):
    n_pages = pt_ref.shape[1]
    past = n_pages * PAGE
    n_blocks = past // MOBA_BLOCK
    slot = _gather_pipeline(pt_ref, n_pages, [(kc_ref, kbuf, ksem), (vc_ref, vbuf, vsem)])
    q = q_ref[...]
    _decode_scores(q, kbuf, slot, past, s_ref)
    lane = lax.broadcasted_iota(jnp.int32, (N_HEADS, LANES), 1)
    gate = jnp.zeros((N_HEADS, LANES), F32)
    for b in range(n_blocks):
        blk_sum = jnp.sum(s_ref[:, b * MOBA_BLOCK:(b + 1) * MOBA_BLOCK], axis=1, keepdims=True)
        gate = jnp.where(lane == b, blk_sum * (1.0 / MOBA_BLOCK), gate)
    sel = _moba_select(gate, n_blocks)
    for b in range(n_blocks):
        win = slice(b * MOBA_BLOCK, (b + 1) * MOBA_BLOCK)
        s_ref[:, win] = s_ref[:, win] + (sel[:, b:b + 1] - 1.0) * (-NEG)
    _decode_softmax_pv(q, vbuf, slot, past, s_ref, knew_ref[...], vnew_ref[...], 0.0, o_ref)


def _dec_attend_call(kernel, name, page_table, q_s, extra, knew, vnew, k_cache, v_cache):
    ns, n_pages = page_table.shape
    past = n_pages * PAGE
    per_seq = lambda shape: pl.BlockSpec((None,) + shape, lambda s, pt: (s, 0, 0))
    in_specs = [per_seq((N_HEADS, LANES))]
    in_specs += [per_seq((1, e.shape[2])) for e in extra]
    in_specs += [per_seq((1, LANES)), per_seq((1, LANES)),
                 pl.BlockSpec(memory_space=pl.ANY), pl.BlockSpec(memory_space=pl.ANY)]
    grid_spec = pltpu.PrefetchScalarGridSpec(
        num_scalar_prefetch=1,
        grid=(ns,),
        in_specs=in_specs,
        out_specs=per_seq((N_HEADS, LANES)),
        scratch_shapes=[pltpu.VMEM((2, LANES, past), F32),
                        pltpu.VMEM((2, LANES, past), F32),
                        pltpu.VMEM((N_HEADS, past), F32),
                        pltpu.SemaphoreType.DMA((2,)), pltpu.SemaphoreType.DMA((2,))])
    return pl.pallas_call(
        kernel,
        grid_spec=grid_spec,
        out_shape=jax.ShapeDtypeStruct((ns, N_HEADS, LANES), F32),
        compiler_params=pltpu.CompilerParams(dimension_semantics=("arbitrary",),
                                             vmem_limit_bytes=48 * 1024 * 1024),
        name=name,
    )(page_table, q_s, *extra, knew, vnew, k_cache, v_cache)


def _heads_from_padded(o):
    return o[:, :, :HEAD_DIM].reshape(o.shape[0], BR_WIDTH)


def _finish_kernel(x_ref, oa_ref, ob_ref, za_ref, zb_ref, ga_ref, gb_ref, ada_ref,
                   wa_ref, wb_ref, wo_ref, gf_ref, y_ref):
    ya = _dot((oa_ref[...] * za_ref[...].astype(F32)).astype(BF16), wa_ref[...])
    yb = _dot((ob_ref[...] * zb_ref[...].astype(F32)).astype(BF16), wb_ref[...])
    m = ga_ref[...].astype(F32) * ya + gb_ref[...].astype(F32) * yb
    gate = ada_ref[...][:, 2 * D_MODEL:]
    xo = x_ref[...] + gate * _dot(m.astype(BF16), wo_ref[...])
    ms = jnp.mean(xo * xo, axis=-1, keepdims=True)
    y_ref[...] = (xo * lax.rsqrt(ms + EPS)) * gf_ref[...]


def _finish(x, oa, ob, za, zb, ga, gb, ada, wa, wb, wo, gf, tm, rows_per_ada):
    rows = x.shape[0]
    tiles_per_ada = rows_per_ada // tm
    a_rows = ada.shape[1]
    row_spec = lambda w: pl.BlockSpec((tm, w), lambda i: (i, 0))
    full = lambda a: pl.BlockSpec(a.shape, lambda i: (0, 0))
    return pl.pallas_call(
        _finish_kernel,
        grid=(rows // tm,),
        in_specs=[row_spec(D_MODEL), row_spec(BR_WIDTH), row_spec(BR_WIDTH),
                  row_spec(BR_WIDTH), row_spec(BR_WIDTH), row_spec(D_MODEL), row_spec(D_MODEL),
                  pl.BlockSpec((None, a_rows, 3 * D_MODEL), lambda i: (i // tiles_per_ada, 0, 0)),
                  full(wa), full(wb), full(wo), full(gf)],
        out_specs=row_spec(D_MODEL),
        out_shape=jax.ShapeDtypeStruct((rows, D_MODEL), F32),
        compiler_params=pltpu.CompilerParams(vmem_limit_bytes=40 * 1024 * 1024),
        name="finish",
    )(x, oa, ob, za, zb, ga, gb, ada, wa, wb, wo, gf)


def _rope_tables(pos):
    half = HEAD_DIM // 2
    freqs = ROPE_THETA ** (-jnp.arange(half, dtype=F32) / half)
    ang = pos.astype(F32)[:, None] * freqs[None, :]
    cos = jnp.cos(ang)
    sin = jnp.sin(ang)
    cos_t = jnp.concatenate([cos, cos, cos, cos], axis=1)
    sin_t = jnp.concatenate([-sin, sin, -sin, sin], axis=1)
    return cos_t, sin_t


def kernel(x_prompt, x_sample, cache_k_a, cache_v_a, cache_k_idx, cache_k_b, cache_v_b, page_table,
           c_prompt, c_sample, w_ada, b_ada, w_in, w_br_a, w_br_b, w_o, g_final):
    n_batch, seq, _ = x_prompt.shape
    ns, dec_seq, _ = x_sample.shape
    depth = w_in.shape[0]
    assert depth == 1 and dec_seq == 1 and seq % MOBA_BLOCK == 0
    n_pages = page_table.shape[1]
    past = n_pages * PAGE
    n_pool = cache_k_a.shape[1]
    tm = MOBA_BLOCK

    pad_rows = (-(n_batch + ns)) % 8
    c_all = jnp.concatenate([c_prompt, c_sample, jnp.zeros((pad_rows, D_MODEL), F32)], axis=0)
    ada = _ada(c_all, w_ada[0], b_ada[0][None, :])
    ada_p = ada[:n_batch].reshape(n_batch, 1, 3 * D_MODEL)
    ada_s = ada[n_batch:n_batch + ns].reshape(1, ns, 3 * D_MODEL)

    w_packed = jnp.concatenate(
        [w_in[0][:, :_ORIG_SPLIT], jnp.zeros((D_MODEL, _PAD_COLS), F32), w_in[0][:, _ORIG_SPLIT:]],
        axis=1).astype(BF16)
    cos_p, sin_p = _rope_tables(jnp.arange(seq, dtype=jnp.int32))
    cos_s, sin_s = _rope_tables(jnp.full((ns,), past, dtype=jnp.int32))

    (qa, qi, qb, ka32, va32, kiw32, kb32, vb32, ka16, va16, kiw16, kb16, vb16,
     za, zb, ga, gb, kmean) = _prepare(x_prompt.reshape(n_batch * seq, D_MODEL), ada_p,
                                       cos_p, sin_p, w_packed, tm, seq)
    oa = _dsa_prompt(qi, qa, kiw32, kiw16, ka16, va16, n_batch, seq)
    n_blk = seq // MOBA_BLOCK
    km_pad = jnp.pad(kmean.reshape(n_batch, n_blk, LANES), ((0, 0), (0, LANES - n_blk), (0, 0)))
    ob = _moba_prompt(qb, kb16, vb16, km_pad, n_batch, seq)
    wa = w_br_a[0].astype(BF16)
    wb = w_br_b[0].astype(BF16)
    wo = w_o[0].astype(BF16)
    gf = g_final[None, :]
    y_p = _finish(x_prompt.reshape(n_batch * seq, D_MODEL), oa, ob, za, zb, ga, gb, ada_p,
                  wa, wb, wo, gf, tm, seq)

    (qa_s, qi_s, qb_s, ka32_s, va32_s, kiw32_s, kb32_s, vb32_s, _, _, _, _, _,
     za_s, zb_s, ga_s, gb_s, _) = _prepare(x_sample.reshape(ns, D_MODEL), ada_s,
                                           cos_s, sin_s, w_packed, ns, ns)
    per_seq_heads = lambda q: jnp.transpose(q, (1, 0, 2))
    row3 = lambda a: a.reshape(ns, 1, LANES)
    w_s = kiw32_s[:, IDX_DIM:IDX_DIM + N_HEADS].reshape(ns, N_HEADS, 1)
    kv_pages = lambda c: jnp.transpose(c[0], (0, 2, 3, 1)).reshape(n_pool, LANES, PAGE)
    idx_pages = jnp.transpose(cache_k_idx[0], (0, 2, 1))
    scores = _dec_scores(page_table, per_seq_heads(qi_s), w_s, row3(kiw32_s), idx_pages)
    bias = _dec_select(scores.reshape(ns, past + CK)).reshape(ns, 1, past + CK)
    oa_s = _dec_attend_call(_dec_dsa_kernel, "dec_dsa", page_table, per_seq_heads(qa_s), [bias],
                            row3(ka32_s), row3(va32_s), kv_pages(cache_k_a), kv_pages(cache_v_a))
    ob_s = _dec_attend_call(_dec_moba_kernel, "dec_moba", page_table, per_seq_heads(qb_s), [],
                            row3(kb32_s), row3(vb32_s), kv_pages(cache_k_b), kv_pages(cache_v_b))
    y_s = _finish(x_sample.reshape(ns, D_MODEL), _heads_from_padded(oa_s), _heads_from_padded(ob_s),
                  za_s, zb_s, ga_s, gb_s, ada_s, wa, wb, wo, gf, ns, ns)

    kv = lambda a, n, t: a.reshape(1, n, t, KV_HEADS, HEAD_DIM)
    ki = lambda a, n, t: a[:, :IDX_DIM].reshape(1, n, t, IDX_DIM)
    return (y_p.reshape(n_batch, seq, D_MODEL), y_s.reshape(ns, 1, D_MODEL),
            kv(ka32, n_batch, seq), kv(va32, n_batch, seq), ki(kiw32, n_batch, seq),
            kv(kb32, n_batch, seq), kv(vb32, n_batch, seq),
            kv(ka32_s, ns, 1), kv(va32_s, ns, 1), ki(kiw32_s, ns, 1),
            kv(kb32_s, ns, 1), kv(vb32_s, ns, 1))
```

```python
import functools

import jax
import jax.numpy as jnp
from jax import lax
from jax.experimental import pallas as pl
from jax.experimental.pallas import tpu as pltpu

F32 = jnp.float32
BF16 = jnp.bfloat16

D_MODEL = 1024
HEAD_DIM = 64
N_HEADS = 8
KV_HEADS = 2
GROUP = N_HEADS // KV_HEADS
IDX_DIM = 64
BR_WIDTH = N_HEADS * HEAD_DIM
DSA_TOPK = 256
MOBA_BLOCK = 256
MOBA_TOPK = 3
PAGE = 128
ROPE_THETA = 10000.0
EPS = 1e-6
LANES = 128
CK = 256
TQ = 256
NEG = -1e30
DEC_CHUNK = 1024

_ORIG_SPLIT = 1864
_PAD_COLS = 56
C_QA, C_KA, C_VA, C_ZA, C_QI, C_KIW, C_QB, C_KB, C_VB, C_ZB, C_GA, C_GB, C_END = (
    0, 512, 640, 768, 1280, 1792, 1920, 2432, 2560, 2688, 3200, 4224, 5248)

LOG2E = 1.4426950408889634
_ATT_WIDTHS = (LANES, 2 * LANES, LANES, 2 * LANES, 2 * LANES)

_NT = (((1,), (1,)), ((), ()))


def _dot_nt(a, b):
    return lax.dot_general(a, b, _NT, preferred_element_type=F32)


def _dot(a, b):
    return jnp.dot(a, b, preferred_element_type=F32)


def _ada_kernel(c_ref, w_ref, b_ref, o_ref):
    o_ref[...] = _dot(c_ref[...].astype(BF16), w_ref[...].astype(BF16)) + b_ref[...]


def _ada(c_all, w_ada, b_ada):
    rows = c_all.shape[0]
    return pl.pallas_call(
        _ada_kernel,
        grid=(3,),
        in_specs=[pl.BlockSpec((rows, D_MODEL), lambda j: (0, 0)),
                  pl.BlockSpec((D_MODEL, D_MODEL), lambda j: (0, j)),
                  pl.BlockSpec((1, D_MODEL), lambda j: (0, j))],
        out_specs=pl.BlockSpec((rows, D_MODEL), lambda j: (0, j)),
        out_shape=jax.ShapeDtypeStruct((rows, 3 * D_MODEL), F32),
        name="ada",
    )(c_all, w_ada, b_ada)


def _rope(x, cos, sin_signed):
    w = x.shape[1]
    reps = w // LANES
    lane = lax.broadcasted_iota(jnp.int32, x.shape, 1)
    upper = (lane % HEAD_DIM) >= (HEAD_DIM // 2)
    from_below = pltpu.roll(x, HEAD_DIM // 2, axis=1)
    from_above = pltpu.roll(x, w - HEAD_DIM // 2, axis=1)
    partner = jnp.where(upper, from_below, from_above)
    if reps > 1:
        cos = jnp.concatenate([cos] * reps, axis=1)
        sin_signed = jnp.concatenate([sin_signed] * reps, axis=1)
    return x * cos + partner * sin_signed


def _store_padded_heads(q, out_ref, low_for_head):
    lane = lax.broadcasted_iota(jnp.int32, (q.shape[0], LANES), 1)
    is_low = lane < HEAD_DIM
    for h in range(N_HEADS):
        col = q[:, (h // 2) * LANES:(h // 2 + 1) * LANES]
        want_low = low_for_head(h)
        if (h % 2 == 0) != want_low:
            col = pltpu.roll(col, HEAD_DIM, axis=1)
        keep = is_low if want_low else jnp.logical_not(is_low)
        out_ref[h] = jnp.where(keep, col, 0.0).astype(BF16)


def _prep_kernel(x_ref, ada_ref, cos_ref, sin_ref, w_ref,
                 qa_ref, qi_ref, qb_ref,
                 ka32_ref, va32_ref, kiw32_ref, kb32_ref, vb32_ref,
                 ka16_ref, va16_ref, kiw16_ref, kb16_ref, vb16_ref,
                 za_ref, zb_ref, ga_ref, gb_ref, km_ref,
                 kat_ref, vat_ref, kiwt_ref, kbt_ref, vbt_ref, *, tiles_per_seq):
    x = x_ref[...]
    ada = ada_ref[...]
    shift = ada[:, :D_MODEL]
    scale = ada[:, D_MODEL:2 * D_MODEL]
    ms = jnp.mean(x * x, axis=-1, keepdims=True)
    h = (x * lax.rsqrt(ms + EPS)) * (1.0 + scale) + shift
    hb = h.astype(BF16)
    cos = cos_ref[...]
    sin = sin_ref[...]

    def proj(a, b):
        return _dot(hb, w_ref[:, a:b])

    kv_low = lambda hh: hh < GROUP
    qscale = HEAD_DIM ** -0.5 * LOG2E
    _store_padded_heads(_rope(proj(C_QA, C_KA), cos, sin) * qscale, qa_ref, kv_low)
    _store_padded_heads(_rope(proj(C_QI, C_KIW), cos, sin) * (IDX_DIM ** -0.5), qi_ref,
                        lambda hh: True)
    _store_padded_heads(_rope(proj(C_QB, C_KB), cos, sin) * qscale, qb_ref, kv_low)

    ka = _rope(proj(C_KA, C_VA), cos, sin)
    ka32_ref[...] = ka
    kat_ref[...] = ka.T
    ka16_ref[...] = ka.astype(BF16)
    va = proj(C_VA, C_ZA)
    va32_ref[...] = va
    vat_ref[...] = va.T
    ones = jnp.ones(va.shape, BF16)
    va16_ref[...] = jnp.concatenate([va.astype(BF16), ones], axis=1)

    kiw = proj(C_KIW, C_QB)
    lane = lax.broadcasted_iota(jnp.int32, kiw.shape, 1)
    kiw = jnp.where(lane < IDX_DIM, _rope(kiw, cos, sin), kiw * (N_HEADS ** -0.5))
    kiw32_ref[...] = kiw
    kiwt_ref[...] = kiw.T
    kiw16_ref[...] = kiw.astype(BF16)

    kb = _rope(proj(C_KB, C_VB), cos, sin)
    kb32_ref[...] = kb
    kbt_ref[...] = kb.T
    own_block = pl.program_id(0) % tiles_per_seq
    block_code = jnp.where(lane == own_block, 1.0, 0.0).astype(BF16)
    kb16_ref[...] = jnp.concatenate([kb.astype(BF16), block_code], axis=1)
    km_ref[0] = jnp.sum(kb, axis=0, keepdims=True) * (1.0 / MOBA_BLOCK)
    vb = proj(C_VB, C_ZB)
    vb32_ref[...] = vb
    vbt_ref[...] = vb.T
    vb16_ref[...] = jnp.concatenate([vb.astype(BF16), ones], axis=1)

    za_ref[...] = jax.nn.silu(proj(C_ZA, C_QI)).astype(BF16)
    zb_ref[...] = jax.nn.silu(proj(C_ZB, C_GA)).astype(BF16)
    ga_ref[...] = jax.nn.sigmoid(proj(C_GA, C_GB)).astype(BF16)
    gb_ref[...] = jax.nn.sigmoid(proj(C_GB, C_END)).astype(BF16)


def _prepare(x, ada, cos, sin, w_packed, tm, rows_per_ada):
    rows = x.shape[0]
    nt = rows // tm
    tab_tiles = cos.shape[0] // tm
    tiles_per_ada = rows_per_ada // tm
    a_rows = ada.shape[1]
    row_spec = lambda w: pl.BlockSpec((tm, w), lambda i: (i, 0))
    head_spec = pl.BlockSpec((N_HEADS, tm, LANES), lambda i: (0, i, 0))
    sd = jax.ShapeDtypeStruct
    out_shape = (
        [sd((N_HEADS, rows, LANES), BF16)] * 3
        + [sd((rows, LANES), F32)] * 5
        + [sd((rows, w), BF16) for w in _ATT_WIDTHS]
        + [sd((rows, BR_WIDTH), BF16)] * 2
        + [sd((rows, D_MODEL), BF16)] * 2
        + [sd((nt, 1, LANES), F32)]
        + [sd((rows // rows_per_ada, LANES, rows_per_ada), F32)] * 5)
    out_specs = (
        [head_spec] * 3 + [row_spec(LANES)] * 5 + [row_spec(w) for w in _ATT_WIDTHS]
        + [row_spec(BR_WIDTH)] * 2
        + [row_spec(D_MODEL)] * 2 + [pl.BlockSpec((1, 1, LANES), lambda i: (i, 0, 0))]
        + [pl.BlockSpec((None, LANES, tm), lambda i: (i // tiles_per_ada, 0, i % tiles_per_ada))] * 5)
    return pl.pallas_call(
        functools.partial(_prep_kernel, tiles_per_seq=tiles_per_ada),
        grid=(nt,),
        in_specs=[row_spec(D_MODEL),
                  pl.BlockSpec((None, a_rows, 3 * D_MODEL), lambda i: (i // tiles_per_ada, 0, 0)),
                  pl.BlockSpec((tm, LANES), lambda i: (i % tab_tiles, 0)),
                  pl.BlockSpec((tm, LANES), lambda i: (i % tab_tiles, 0)),
                  pl.BlockSpec((D_MODEL, C_END), lambda i: (0, 0))],
        out_specs=out_specs,
        out_shape=out_shape,
        compiler_params=pltpu.CompilerParams(vmem_limit_bytes=56 * 1024 * 1024),
        name="prepare",
    )(x, ada, cos, sin, w_packed)


def _topk_threshold(sc_ref, nc, kcol, nvalid):
    rows = sc_ref.shape[1]
    n_pairs = (nc + 1) // 2

    band = 128
    bands = [slice(r, r + band) for r in range(0, rows, band)]
    halves = (slice(0, LANES), slice(LANES, CK))

    def count_gt(tcol):
        parts = []
        for rs in bands:
            tb = jnp.broadcast_to(tcol[rs], (band, LANES))

            def body(i, acc, rs=rs, tb=tb):
                for c in (2 * i, 2 * i + 1):
                    for ls in halves:
                        acc = acc + jnp.where(sc_ref[c, rs, ls] > tb, 1.0, 0.0)
                return acc

            acc = lax.fori_loop(0, n_pairs, body, jnp.zeros((band, LANES), F32))
            parts.append(jnp.sum(acc, axis=1, keepdims=True))
        return jnp.concatenate(parts, axis=0)

    pos_parts, nonneg_parts, neg_parts = [], [], []
    for rs in bands:
        def zero_stats(c, carry, rs=rs):
            n_pos, n_nonneg, neg_max = carry
            for ls in halves:
                s = sc_ref[c, rs, ls]
                n_pos = n_pos + jnp.where(s > 0.0, 1.0, 0.0)
                n_nonneg = n_nonneg + jnp.where(s >= 0.0, 1.0, 0.0)
                neg_max = jnp.maximum(neg_max, jnp.where(s < 0.0, s, -jnp.inf))
            return n_pos, n_nonneg, neg_max

        zl = jnp.zeros((band, LANES), F32)
        n_pos, n_nonneg, neg_max = lax.fori_loop(
            0, nc, zero_stats, (zl, zl, jnp.full((band, LANES), -jnp.inf, F32)))
        pos_parts.append(jnp.sum(n_pos, axis=1, keepdims=True))
        nonneg_parts.append(jnp.sum(n_nonneg, axis=1, keepdims=True))
        neg_parts.append(jnp.max(neg_max, axis=1, keepdims=True))
    c_pos = jnp.concatenate(pos_parts, axis=0)
    c_nonneg = jnp.concatenate(nonneg_parts, axis=0)
    neg_max = jnp.concatenate(neg_parts, axis=0)

    max_parts, min_parts = [], []
    for rs in bands:
        def extremes(c, carry, rs=rs):
            vmax, vmin = carry
            for ls in halves:
                s = sc_ref[c, rs, ls]
                vmax = jnp.maximum(vmax, s)
                vmin = jnp.minimum(vmin, jnp.where(s > -jnp.inf, s, jnp.inf))
            return vmax, vmin

        vmax, vmin = lax.fori_loop(0, nc, extremes, (jnp.full((band, LANES), -jnp.inf, F32),
                                                     jnp.full((band, LANES), jnp.inf, F32)))
        max_parts.append(jnp.max(vmax, axis=1, keepdims=True))
        min_parts.append(jnp.min(vmin, axis=1, keepdims=True))
    row_max = jnp.concatenate(max_parts, axis=0)
    row_min = jnp.concatenate(min_parts, axis=0)

    lo0 = row_min - 1.0 - jnp.abs(row_min) * (2.0 ** -10)
    below_zero = jnp.where(neg_max > -jnp.inf, neg_max, lo0)
    zeros = jnp.zeros_like(row_max)
    take_all = nvalid <= kcol
    open_row = jnp.logical_not(take_all)
    hit_zero = jnp.logical_and(open_row, c_pos == kcol)
    tie_zero = jnp.logical_and(open_row, jnp.logical_and(c_pos < kcol, c_nonneg >= kcol))
    above = jnp.logical_and(open_row, c_pos > kcol)
    below = jnp.logical_and(open_row, c_nonneg < kcol)
    searching = jnp.logical_or(above, below)
    lo_init = jnp.where(jnp.logical_or(above, hit_zero), 0.0, jnp.where(tie_zero, below_zero, lo0))
    hi_init = jnp.where(above, row_max, jnp.where(below, below_zero, 0.0))
    state0 = jnp.where(searching, 0.0, jnp.where(tie_zero, 2.0, 1.0))
    n_live0 = jnp.sum(jnp.where(searching, 1.0, 0.0)).astype(jnp.int32)
    init = (lo_init, hi_init, state0, n_live0, jnp.int32(0))

    def cond(st):
        return jnp.logical_and(st[3] > 0, st[4] < 400)

    def step(st):
        lo, hi, state, _, it = st
        mid = lo + 0.5 * (hi - lo)
        c = count_gt(mid)
        live = state < 0.5
        hit = c == kcol
        spent = jnp.logical_or(mid <= lo, mid >= hi)
        finish_hit = jnp.logical_and(live, hit)
        finish_tie = jnp.logical_and(live, jnp.logical_and(spent, jnp.logical_not(hit)))
        keep_going = jnp.logical_and(live, jnp.logical_not(finish_tie))
        lo = jnp.where(jnp.logical_and(keep_going, c >= kcol), mid, lo)
        hi = jnp.where(jnp.logical_and(keep_going, c < kcol), mid, hi)
        state = jnp.where(finish_hit, 1.0, jnp.where(finish_tie, 2.0, state))
        n_live = jnp.sum(jnp.where(state < 0.5, 1.0, 0.0)).astype(jnp.int32)
        return (lo, hi, state, n_live, it + 1)

    def body(st):
        return step(step(st))

    lo, hi, state, _, _ = lax.while_loop(cond, body, init)
    t = lo
    tie = jnp.where(state > 1.5, 1.0, 0.0)

    @pl.when(jnp.sum(tie) > 0.5)
    def _():
        need = kcol - count_gt(hi)
        ri = lax.broadcasted_iota(jnp.int32, (CK, CK), 0)
        ci = lax.broadcasted_iota(jnp.int32, (CK, CK), 1)
        tri = jnp.where(ri <= ci, 1.0, 0.0).astype(BF16)
        hib = jnp.broadcast_to(hi, (rows, CK))
        tieb = jnp.broadcast_to(tie, (rows, CK)) > 0.5

        def fix(c, before):
            s = sc_ref[c]
            eq = jnp.where(jnp.logical_and(s == hib, tieb), 1.0, 0.0)
            rank = before + _dot(eq.astype(BF16), tri)
            drop = jnp.logical_and(eq > 0.5, rank > need)
            sc_ref[c] = jnp.where(drop, -jnp.inf, s)
            return before + jnp.sum(eq, axis=1, keepdims=True)

        lax.fori_loop(0, nc, fix, jnp.zeros_like(hi))

    return t


def _scores(q_ref, kc):
    q = q_ref[...].reshape(N_HEADS * TQ, q_ref.shape[2])
    return _dot_nt(q, kc).reshape(N_HEADS, TQ, CK)


def _track_max(s, mrun_ref):
    mrun_ref[...] = jnp.maximum(mrun_ref[...], jnp.maximum(s[:, :, :LANES], s[:, :, LANES:]))


def _finish_max(mrun_ref, mb_ref):
    m = jnp.max(mrun_ref[...], axis=2, keepdims=True)
    mb_ref[...] = jnp.broadcast_to(m, mb_ref.shape)


def _accumulate_pv(s, vc, mb_ref, acc_ref):
    p = jnp.exp2(s - mb_ref[...]).astype(BF16).reshape(N_HEADS * TQ, CK)
    acc_ref[...] = acc_ref[...] + _dot(p, vc).reshape(acc_ref.shape)


def _two_pass_attention(n_chunks, q_ref, k_ref, v_ref, bias_of_chunk, mrun_ref, mb_ref, acc_ref):
    mrun_ref[...] = jnp.full(mrun_ref.shape, NEG, F32)
    acc_ref[...] = jnp.zeros(acc_ref.shape, F32)
    n_pairs = (n_chunks + 1) // 2

    def max_pair(i, carry):
        for j in range(2):
            c = 2 * i + j
            kc = k_ref[pl.ds(pl.multiple_of(c * CK, CK), CK), :]
            _track_max(_scores(q_ref, kc) + bias_of_chunk(c)[None], mrun_ref)
        return carry

    lax.fori_loop(0, n_pairs, max_pair, 0)
    _finish_max(mrun_ref, mb_ref)

    def attend_pair(i, carry):
        for j in range(2):
            c = 2 * i + j
            off = pl.multiple_of(c * CK, CK)
            _accumulate_pv(_scores(q_ref, k_ref[pl.ds(off, CK), :]) + bias_of_chunk(c)[None],
                           v_ref[pl.ds(off, CK), :], mb_ref, acc_ref)
        return carry

    lax.fori_loop(0, n_pairs, attend_pair, 0)


def _write_heads(o_ref, acc_ref):
    lane = lax.broadcasted_iota(jnp.int32, (o_ref.shape[0], LANES), 1)
    is_low = lane < HEAD_DIM
    for pair in range(N_HEADS // 2):
        halves = []
        for h in (2 * pair, 2 * pair + 1):
            acc = acc_ref[h]
            o = acc[:, :LANES] / acc[:, LANES:]
            in_low = h < GROUP
            want_low = h % 2 == 0
            if in_low != want_low:
                o = pltpu.roll(o, HEAD_DIM, axis=1)
            halves.append(o)
        o_ref[:, pair * LANES:(pair + 1) * LANES] = jnp.where(is_low, halves[0], halves[1])


def _dsa_prompt_kernel(qi_ref, qa_ref, wt_ref, kiw_ref, ka_ref, va_ref, oa_ref,
                       sc_ref, wb_ref, mrun_ref, mb_ref, acc_ref):
    qi = pl.program_id(1)
    nc = (qi * TQ + TQ + CK - 1) // CK
    qpos = qi * TQ + lax.broadcasted_iota(jnp.int32, (TQ, CK), 0)
    lane = lax.broadcasted_iota(jnp.int32, (TQ, CK), 1)

    wt = wt_ref[...]
    for h in range(N_HEADS):
        wb_ref[h] = jnp.broadcast_to(wt[:, IDX_DIM + h:IDX_DIM + h + 1], (TQ, CK))

    def score_pair(i, carry):
        for c in (2 * i, 2 * i + 1):
            kc = kiw_ref[pl.ds(pl.multiple_of(c * CK, CK), CK), :]
            acc = jnp.sum(wb_ref[...] * jnp.maximum(_scores(qi_ref, kc), 0.0), axis=0)
            sc_ref[c] = jnp.where((c * CK + lane) <= qpos, acc, -jnp.inf)
        return carry

    lax.fori_loop(0, (nc + 1) // 2, score_pair, 0)
    nvalid = (qpos[:, :1] + 1).astype(F32)
    kcol = jnp.minimum(nvalid, float(DSA_TOPK))
    t = _topk_threshold(sc_ref, nc, kcol, nvalid)

    tb = jnp.broadcast_to(t, (TQ, CK))

    def to_bias(c, carry):
        sc_ref[c] = jnp.where(sc_ref[c] > tb, 0.0, NEG)
        return carry

    lax.fori_loop(0, nc, to_bias, 0)

    @pl.when(nc % 2 == 1)
    def _():
        sc_ref[nc] = jnp.full((TQ, CK), NEG, F32)

    _two_pass_attention(nc, qa_ref, ka_ref, va_ref, lambda c: sc_ref[c], mrun_ref, mb_ref, acc_ref)
    _write_heads(oa_ref, acc_ref)


def _dsa_prompt(qi, qa, kiw32, kiw16, ka16, va16, n_batch, seq):
    nq = seq // TQ
    head_spec = pl.BlockSpec((N_HEADS, TQ, LANES), lambda n, i: (0, n * nq + i, 0))
    seq_spec = pl.BlockSpec((seq, LANES), lambda n, i: (n, 0))
    return pl.pallas_call(
        _dsa_prompt_kernel,
        grid=(n_batch, nq),
        in_specs=[head_spec, head_spec,
                  pl.BlockSpec((TQ, LANES), lambda n, i: (n * nq + i, 0)),
                  seq_spec, seq_spec,
                  pl.BlockSpec((seq, 2 * LANES), lambda n, i: (n, 0))],
        out_specs=pl.BlockSpec((TQ, BR_WIDTH), lambda n, i: (n * nq + i, 0)),
        out_shape=jax.ShapeDtypeStruct((n_batch * seq, BR_WIDTH), F32),
        scratch_shapes=[pltpu.VMEM((seq // CK, TQ, CK), F32),
                        pltpu.VMEM((N_HEADS, TQ, CK), F32),
                        pltpu.VMEM((N_HEADS, TQ, LANES), F32),
                        pltpu.VMEM((N_HEADS, TQ, CK), F32),
                        pltpu.VMEM((N_HEADS, TQ, 2 * LANES), F32)],
        compiler_params=pltpu.CompilerParams(vmem_limit_bytes=56 * 1024 * 1024),
        name="dsa_prompt",
    )(qi, qa, kiw32, kiw16, ka16, va16)


def _moba_select(gate, n_past):
    blk = lax.broadcasted_iota(jnp.int32, gate.shape, 1)
    blkf = blk.astype(F32)
    past = blk < n_past
    g = jnp.where(past, gate, -jnp.inf)
    sel = jnp.zeros(gate.shape, F32)
    for _ in range(MOBA_TOPK):
        top = jnp.max(g, axis=1, keepdims=True)
        first = jnp.min(jnp.where(g == top, blkf, float(LANES)), axis=1, keepdims=True)
        pick = blkf == first
        sel = jnp.where(pick, 1.0, sel)
        g = jnp.where(pick, -jnp.inf, g)
    return jnp.where(past, sel, 0.0)


def _moba_prompt_kernel(qb_ref, kb_ref, vb_ref, km_ref, ob_ref, q2_ref, mrun_ref, mb_ref, acc_ref):
    qi = pl.program_id(1)
    own = (qi * TQ) // MOBA_BLOCK
    km = km_ref[...].astype(BF16)
    blk = lax.broadcasted_iota(jnp.int32, (TQ, LANES), 1)
    for h in range(N_HEADS):
        q = qb_ref[h]
        sel = _moba_select(_dot_nt(q, km), own)
        allowed = jnp.logical_or(sel > 0.5, blk >= own)
        q2_ref[h] = jnp.concatenate([q, jnp.where(allowed, 0.0, NEG).astype(BF16)], axis=1)

    qpos = qi * TQ + lax.broadcasted_iota(jnp.int32, (TQ, CK), 0)
    lane = lax.broadcasted_iota(jnp.int32, (TQ, CK), 1)
    causal_bias = jnp.where((own * CK + lane) <= qpos, 0.0, NEG)

    def bias_of_block(b):
        return jnp.where(b < own, 0.0, jnp.where(b == own, causal_bias, NEG))

    _two_pass_attention(own + 1, q2_ref, kb_ref, vb_ref, bias_of_block, mrun_ref, mb_ref, acc_ref)
    _write_heads(ob_ref, acc_ref)


def _moba_prompt(qb, kb16, vb16, kmean, n_batch, seq):
    nq = seq // TQ
    head_spec = pl.BlockSpec((N_HEADS, TQ, LANES), lambda n, i: (0, n * nq + i, 0))
    seq_spec = pl.BlockSpec((seq, 2 * LANES), lambda n, i: (n, 0))
    return pl.pallas_call(
        _moba_prompt_kernel,
        grid=(n_batch, nq),
        in_specs=[head_spec, seq_spec, seq_spec,
                  pl.BlockSpec((None, LANES, LANES), lambda n, i: (n, 0, 0))],
        out_specs=pl.BlockSpec((TQ, BR_WIDTH), lambda n, i: (n * nq + i, 0)),
        out_shape=jax.ShapeDtypeStruct((n_batch * seq, BR_WIDTH), F32),
        scratch_shapes=[pltpu.VMEM((N_HEADS, TQ, 2 * LANES), BF16),
                        pltpu.VMEM((N_HEADS, TQ, LANES), F32),
                        pltpu.VMEM((N_HEADS, TQ, CK), F32),
                        pltpu.VMEM((N_HEADS, TQ, 2 * LANES), F32)],
        compiler_params=pltpu.CompilerParams(vmem_limit_bytes=40 * 1024 * 1024),
        name="moba_prompt",
    )(qb, kb16, vb16, kmean)


def _page_copy(cache_ref, buf_ref, sem_ref, page, slot, p):
    window = pl.ds(pl.multiple_of(p * PAGE, PAGE), PAGE)
    return pltpu.make_async_copy(cache_ref.at[page], buf_ref.at[slot, :, window], sem_ref.at[slot])


def _start_pages(pt_ref, seq_idx, slot, n_pages, streams):
    def body(p, carry):
        page = pt_ref[seq_idx, p]
        for cache_ref, buf_ref, sem_ref in streams:
            _page_copy(cache_ref, buf_ref, sem_ref, page, slot, p).start()
        return carry

    lax.fori_loop(0, n_pages, body, 0)


def _wait_pages(slot, n_pages, streams):
    def body(p, carry):
        for cache_ref, buf_ref, sem_ref in streams:
            _page_copy(cache_ref, buf_ref, sem_ref, 0, slot, p).wait()
        return carry

    lax.fori_loop(0, n_pages, body, 0)


def _gather_pipeline(pt_ref, n_pages, streams):
    s = pl.program_id(0)
    slot = s % 2

    @pl.when(s == 0)
    def _():
        _start_pages(pt_ref, s, slot, n_pages, streams)

    @pl.when(s + 1 < pl.num_programs(0))
    def _():
        _start_pages(pt_ref, s + 1, 1 - slot, n_pages, streams)

    _wait_pages(slot, n_pages, streams)
    return slot


def _dec_scores_kernel(pt_ref, q_ref, w_ref, knew_ref, cache_ref, o_ref, kbuf, sem):
    n_pages = pt_ref.shape[1]
    past = n_pages * PAGE
    slot = _gather_pipeline(pt_ref, n_pages, [(cache_ref, kbuf, sem)])
    q = q_ref[...]
    w = w_ref[...]
    qk = q[:, :IDX_DIM]
    for ch in range(past // DEC_CHUNK):
        kc = kbuf[slot, :, ch * DEC_CHUNK:(ch + 1) * DEC_CHUNK].astype(BF16)
        lg = _dot(qk, kc)
        o_ref[:, ch * DEC_CHUNK:(ch + 1) * DEC_CHUNK] = jnp.sum(
            w * jnp.maximum(lg, 0.0), axis=0, keepdims=True)
    knew = knew_ref[...].astype(BF16).astype(F32)
    lg_new = jnp.sum(q.astype(F32) * knew, axis=1, keepdims=True)
    s_new = jnp.sum(w * jnp.maximum(lg_new, 0.0), axis=0, keepdims=True)
    lane = lax.broadcasted_iota(jnp.int32, (1, CK), 1)
    o_ref[:, past:past + CK] = jnp.where(lane == 0, s_new, -jnp.inf)


def _dec_scores(page_table, qi_s, w_s, knew, cache):
    ns, n_pages = page_table.shape
    past = n_pages * PAGE
    width = past + CK
    grid_spec = pltpu.PrefetchScalarGridSpec(
        num_scalar_prefetch=1,
        grid=(ns,),
        in_specs=[pl.BlockSpec((None, N_HEADS, LANES), lambda s, pt: (s, 0, 0)),
                  pl.BlockSpec((None, N_HEADS, 1), lambda s, pt: (s, 0, 0)),
                  pl.BlockSpec((None, 1, LANES), lambda s, pt: (s, 0, 0)),
                  pl.BlockSpec(memory_space=pl.ANY)],
        out_specs=pl.BlockSpec((None, 1, width), lambda s, pt: (s, 0, 0)),
        scratch_shapes=[pltpu.VMEM((2, IDX_DIM, past), F32),
                        pltpu.SemaphoreType.DMA((2,))])
    return pl.pallas_call(
        _dec_scores_kernel,
        grid_spec=grid_spec,
        out_shape=jax.ShapeDtypeStruct((ns, 1, width), F32),
        compiler_params=pltpu.CompilerParams(dimension_semantics=("arbitrary",),
                                             vmem_limit_bytes=32 * 1024 * 1024),
        name="dec_scores",
    )(page_table, qi_s, w_s, knew, cache)


def _dec_select_kernel(s_ref, bias_ref, sc_ref):
    rows, width = s_ref.shape
    nch = width // CK
    for c in range(nch):
        sc_ref[c] = s_ref[:, c * CK:(c + 1) * CK]
    for c in range(nch, sc_ref.shape[0]):
        sc_ref[c] = jnp.full((rows, CK), -jnp.inf, F32)
    nvalid = jnp.full((rows, 1), float(width - CK + 1), F32)
    kcol = jnp.minimum(nvalid, float(DSA_TOPK))
    t = _topk_threshold(sc_ref, nch, kcol, nvalid)
    tb = jnp.broadcast_to(t, (rows, CK))
    for c in range(nch):
        bias_ref[:, c * CK:(c + 1) * CK] = jnp.where(sc_ref[c] > tb, 0.0, NEG)


def _dec_select(scores):
    rows, width = scores.shape
    return pl.pallas_call(
        _dec_select_kernel,
        grid=(1,),
        in_specs=[pl.BlockSpec((rows, width), lambda i: (0, 0))],
        out_specs=pl.BlockSpec((rows, width), lambda i: (0, 0)),
        out_shape=jax.ShapeDtypeStruct((rows, width), F32),
        scratch_shapes=[pltpu.VMEM((2 * ((width // CK + 1) // 2), rows, CK), F32)],
        compiler_params=pltpu.CompilerParams(vmem_limit_bytes=40 * 1024 * 1024),
        name="dec_select",
    )(scores)


def _decode_scores(q, kbuf, slot, past, s_ref):
    for ch in range(past // DEC_CHUNK):
        win = slice(ch * DEC_CHUNK, (ch + 1) * DEC_CHUNK)
        s_ref[:, win] = _dot(q, kbuf[slot, :, win].astype(BF16))


def _decode_softmax_pv(q, vbuf, slot, past, s_ref, knew, vnew, bias_new, o_ref):
    n_ch = past // DEC_CHUNK
    s_new = jnp.sum(q.astype(F32) * knew.astype(BF16).astype(F32), axis=1, keepdims=True) + bias_new
    m = s_new
    for ch in range(n_ch):
        win = slice(ch * DEC_CHUNK, (ch + 1) * DEC_CHUNK)
        m = jnp.maximum(m, jnp.max(s_ref[:, win], axis=1, keepdims=True))
    p_new = jnp.exp2(s_new - m)
    l = p_new
    acc = p_new * vnew.astype(BF16).astype(F32)
    for ch in range(n_ch):
        win = slice(ch * DEC_CHUNK, (ch + 1) * DEC_CHUNK)
        p = jnp.exp2(s_ref[:, win] - m)
        l = l + jnp.sum(p, axis=1, keepdims=True)
        acc = acc + _dot_nt(p.astype(BF16), vbuf[slot, :, win].astype(BF16))
    o = acc / l
    head = lax.broadcasted_iota(jnp.int32, o.shape, 0)
    o_ref[...] = jnp.where(head < GROUP, o, pltpu.roll(o, HEAD_DIM, axis=1))


def _dec_dsa_kernel(pt_ref, q_ref, bias_ref, knew_ref, vnew_ref, kc_ref, vc_ref, o_ref,
                    kbuf, vbuf, s_ref, ksem, vsem):
    n_pages = pt_ref.shape[1]
    past = n_pages * PAGE
    slot = _gather_pipeline(pt_ref, n_pages, [(kc_ref, kbuf, ksem), (vc_ref, vbuf, vsem)])
    q = q_ref[...]
    _decode_scores(q, kbuf, slot, past, s_ref)
    for ch in range(past // DEC_CHUNK):
        win = slice(ch * DEC_CHUNK, (ch + 1) * DEC_CHUNK)
        s_ref[:, win] = s_ref[:, win] + bias_ref[:, win]
    _decode_softmax_pv(q, vbuf, slot, past, s_ref, knew_ref[...], vnew_ref[...],
                       bias_ref[:, past:past + 1], o_ref)


def _dec_moba_kernel(pt_ref, q_ref, knew_ref, vnew_ref, kc_ref, vc_ref, o_ref,
                     kbuf, vbuf, s_ref, ksem, vsem):
    n_pages = pt_ref.shape[1]
    past = n_pages * PAGE
    n_blocks = past // MOBA_BLOCK
    slot = _gather_pipeline(pt_ref, n_pages, [(kc_ref, kbuf, ksem), (vc_ref, vbuf, vsem)])
    q = q_ref[...]
    _decode_scores(q, kbuf, slot, past, s_ref)
    lane = lax.broadcasted_iota(jnp.int32, (N_HEADS, LANES), 1)
    gate = jnp.zeros((N_HEADS, LANES), F32)
    for b in range(n_blocks):
        blk_sum = jnp.sum(s_ref[:, b * MOBA_BLOCK:(b + 1) * MOBA_BLOCK], axis=1, keepdims=True)
        gate = jnp.where(lane == b, blk_sum * (1.0 / MOBA_BLOCK), gate)
    sel = _moba_select(gate, n_blocks)
    for b in range(n_blocks):
        win = slice(b * MOBA_BLOCK, (b + 1) * MOBA_BLOCK)
        s_ref[:, win] = s_ref[:, win] + (sel[:, b:b + 1] - 1.0) * (-NEG)
    _decode_softmax_pv(q, vbuf, slot, past, s_ref, knew_ref[...], vnew_ref[...], 0.0, o_ref)


def _dec_attend_call(kernel, name, page_table, q_s, extra, knew, vnew, k_cache, v_cache):
    ns, n_pages = page_table.shape
    past = n_pages * PAGE
    per_seq = lambda shape: pl.BlockSpec((None,) + shape, lambda s, pt: (s, 0, 0))
    in_specs = [per_seq((N_HEADS, LANES))]
    in_specs += [per_seq((1, e.shape[2])) for e in extra]
    in_specs += [per_seq((1, LANES)), per_seq((1, LANES)),
                 pl.BlockSpec(memory_space=pl.ANY), pl.BlockSpec(memory_space=pl.ANY)]
    grid_spec = pltpu.PrefetchScalarGridSpec(
        num_scalar_prefetch=1,
        grid=(ns,),
        in_specs=in_specs,
        out_specs=per_seq((N_HEADS, LANES)),
        scratch_shapes=[pltpu.VMEM((2, LANES, past), F32),
                        pltpu.VMEM((2, LANES, past), F32),
                        pltpu.VMEM((N_HEADS, past), F32),
                        pltpu.SemaphoreType.DMA((2,)), pltpu.SemaphoreType.DMA((2,))])
    return pl.pallas_call(
        kernel,
        grid_spec=grid_spec,
        out_shape=jax.ShapeDtypeStruct((ns, N_HEADS, LANES), F32),
        compiler_params=pltpu.CompilerParams(dimension_semantics=("arbitrary",),
                                             vmem_limit_bytes=48 * 1024 * 1024),
        name=name,
    )(page_table, q_s, *extra, knew, vnew, k_cache, v_cache)


def _heads_from_padded(o):
    return o[:, :, :HEAD_DIM].reshape(o.shape[0], BR_WIDTH)


def _finish_kernel(x_ref, oa_ref, ob_ref, za_ref, zb_ref, ga_ref, gb_ref, ada_ref,
                   wa_ref, wb_ref, wo_ref, gf_ref, y_ref):
    ya = _dot((oa_ref[...] * za_ref[...].astype(F32)).astype(BF16), wa_ref[...])
    yb = _dot((ob_ref[...] * zb_ref[...].astype(F32)).astype(BF16), wb_ref[...])
    m = ga_ref[...].astype(F32) * ya + gb_ref[...].astype(F32) * yb
    gate = ada_ref[...][:, 2 * D_MODEL:]
    xo = x_ref[...] + gate * _dot(m.astype(BF16), wo_ref[...])
    ms = jnp.mean(xo * xo, axis=-1, keepdims=True)
    y_ref[...] = (xo * lax.rsqrt(ms + EPS)) * gf_ref[...]


def _finish(x, oa, ob, za, zb, ga, gb, ada, wa, wb, wo, gf, tm, rows_per_ada):
    rows = x.shape[0]
    tiles_per_ada = rows_per_ada // tm
    a_rows = ada.shape[1]
    row_spec = lambda w: pl.BlockSpec((tm, w), lambda i: (i, 0))
    full = lambda a: pl.BlockSpec(a.shape, lambda i: (0, 0))
    return pl.pallas_call(
        _finish_kernel,
        grid=(rows // tm,),
        in_specs=[row_spec(D_MODEL), row_spec(BR_WIDTH), row_spec(BR_WIDTH),
                  row_spec(BR_WIDTH), row_spec(BR_WIDTH), row_spec(D_MODEL), row_spec(D_MODEL),
                  pl.BlockSpec((None, a_rows, 3 * D_MODEL), lambda i: (i // tiles_per_ada, 0, 0)),
                  full(wa), full(wb), full(wo), full(gf)],
        out_specs=row_spec(D_MODEL),
        out_shape=jax.ShapeDtypeStruct((rows, D_MODEL), F32),
        compiler_params=pltpu.CompilerParams(vmem_limit_bytes=40 * 1024 * 1024),
        name="finish",
    )(x, oa, ob, za, zb, ga, gb, ada, wa, wb, wo, gf)


def _rope_tables(pos):
    half = HEAD_DIM // 2
    freqs = ROPE_THETA ** (-jnp.arange(half, dtype=F32) / half)
    ang = pos.astype(F32)[:, None] * freqs[None, :]
    cos = jnp.cos(ang)
    sin = jnp.sin(ang)
    cos_t = jnp.concatenate([cos, cos, cos, cos], axis=1)
    sin_t = jnp.concatenate([-sin, sin, -sin, sin], axis=1)
    return cos_t, sin_t


def kernel(x_prompt, x_sample, cache_k_a, cache_v_a, cache_k_idx, cache_k_b, cache_v_b, page_table,
           c_prompt, c_sample, w_ada, b_ada, w_in, w_br_a, w_br_b, w_o, g_final):
    n_batch, seq, _ = x_prompt.shape
    ns, dec_seq, _ = x_sample.shape
    depth = w_in.shape[0]
    assert depth == 1 and dec_seq == 1 and seq % MOBA_BLOCK == 0
    n_pages = page_table.shape[1]
    past = n_pages * PAGE
    n_pool = cache_k_a.shape[1]
    tm = MOBA_BLOCK

    pad_rows = (-(n_batch + ns)) % 8
    c_all = jnp.concatenate([c_prompt, c_sample, jnp.zeros((pad_rows, D_MODEL), F32)], axis=0)
    ada = _ada(c_all, w_ada[0], b_ada[0][None, :])
    ada_p = ada[:n_batch].reshape(n_batch, 1, 3 * D_MODEL)
    ada_s = ada[n_batch:n_batch + ns].reshape(1, ns, 3 * D_MODEL)

    w_packed = jnp.concatenate(
        [w_in[0][:, :_ORIG_SPLIT], jnp.zeros((D_MODEL, _PAD_COLS), F32), w_in[0][:, _ORIG_SPLIT:]],
        axis=1).astype(BF16)
    cos_p, sin_p = _rope_tables(jnp.arange(seq, dtype=jnp.int32))
    cos_s, sin_s = _rope_tables(jnp.full((ns,), past, dtype=jnp.int32))

    (qa, qi, qb, ka32, va32, kiw32, kb32, vb32, ka16, va16, kiw16, kb16, vb16,
     za, zb, ga, gb, kmean, kat, vat, kiwt, kbt, vbt) = _prepare(x_prompt.reshape(n_batch * seq, D_MODEL), ada_p,
                                       cos_p, sin_p, w_packed, tm, seq)
    oa = _dsa_prompt(qi, qa, kiw32, kiw16, ka16, va16, n_batch, seq)
    n_blk = seq // MOBA_BLOCK
    km_pad = jnp.pad(kmean.reshape(n_batch, n_blk, LANES), ((0, 0), (0, LANES - n_blk), (0, 0)))
    ob = _moba_prompt(qb, kb16, vb16, km_pad, n_batch, seq)
    wa = w_br_a[0].astype(BF16)
    wb = w_br_b[0].astype(BF16)
    wo = w_o[0].astype(BF16)
    gf = g_final[None, :]
    y_p = _finish(x_prompt.reshape(n_batch * seq, D_MODEL), oa, ob, za, zb, ga, gb, ada_p,
                  wa, wb, wo, gf, tm, seq)

    (qa_s, qi_s, qb_s, ka32_s, va32_s, kiw32_s, kb32_s, vb32_s, _, _, _, _, _,
     za_s, zb_s, ga_s, gb_s, _, _, _, _, _, _) = _prepare(x_sample.reshape(ns, D_MODEL), ada_s,
                                           cos_s, sin_s, w_packed, ns, ns)
    per_seq_heads = lambda q: jnp.transpose(q, (1, 0, 2))
    row3 = lambda a: a.reshape(ns, 1, LANES)
    w_s = kiw32_s[:, IDX_DIM:IDX_DIM + N_HEADS].reshape(ns, N_HEADS, 1)
    kv_pages = lambda c: jnp.transpose(c[0], (0, 2, 3, 1)).reshape(n_pool, LANES, PAGE)
    idx_pages = jnp.transpose(cache_k_idx[0], (0, 2, 1))
    scores = _dec_scores(page_table, per_seq_heads(qi_s), w_s, row3(kiw32_s), idx_pages)
    bias = _dec_select(scores.reshape(ns, past + CK)).reshape(ns, 1, past + CK)
    oa_s = _dec_attend_call(_dec_dsa_kernel, "dec_dsa", page_table, per_seq_heads(qa_s), [bias],
                            row3(ka32_s), row3(va32_s), kv_pages(cache_k_a), kv_pages(cache_v_a))
    ob_s = _dec_attend_call(_dec_moba_kernel, "dec_moba", page_table, per_seq_heads(qb_s), [],
                            row3(kb32_s), row3(vb32_s), kv_pages(cache_k_b), kv_pages(cache_v_b))
    y_s = _finish(x_sample.reshape(ns, D_MODEL), _heads_from_padded(oa_s), _heads_from_padded(ob_s),
                  za_s, zb_s, ga_s, gb_s, ada_s, wa, wb, wo, gf, ns, ns)

    kv = lambda a, n, t: a.reshape(1, n, t, KV_HEADS, HEAD_DIM)
    kvt = lambda a: jnp.transpose(a.reshape(n_batch, KV_HEADS, HEAD_DIM, seq), (0, 3, 1, 2))[None]
    ki = lambda a, n, t: a[:, :IDX_DIM].reshape(1, n, t, IDX_DIM)
    return (y_p.reshape(n_batch, seq, D_MODEL), y_s.reshape(ns, 1, D_MODEL),
            kvt(kat), kvt(vat), jnp.transpose(kiwt[:, :IDX_DIM, :], (0, 2, 1))[None],
            kvt(kbt), kvt(vbt),
            kv(ka32_s, ns, 1), kv(va32_s, ns, 1), ki(kiw32_s, ns, 1),
            kv(kb32_s, ns, 1), kv(vb32_s, ns, 1))
```
